```python
import math
import jax, jax.numpy as jnp
from jax import lax
import numpy as np

D_MODEL = 1024
BATCH = 8
SEQ = 2048
DEPTH = 2

N_A_LAYERS = DEPTH // 2
N_B_LAYERS = DEPTH - N_A_LAYERS
D_FF = 2816
LRU_WIDTH = D_MODEL
LRU_BLOCKS = 4
LRU_BLOCK_WIDTH = LRU_WIDTH // LRU_BLOCKS
CONV_WIDTH = 4
RG_C = 8.0
N_HEADS = 8
QK_DIM = 64
V_DIM = 2 * QK_DIM
KV_WIDTH = N_HEADS * (2 * QK_DIM + V_DIM)
ROPE_THETA = 10000.0
Q_BLOCK = 128
EPS = 1e-6

kernel_name = "yoco_hawk_diffattn_macaron"


def rmsnorm(x, g):
    xf = x.astype(jnp.float32)
    y = xf * lax.rsqrt(jnp.mean(xf * xf, axis=-1, keepdims=True) + EPS)
    return (y * g.astype(jnp.float32)).astype(x.dtype)


def swiglu_ffn(u, w_in, w_out):
    gate, up = jnp.split(u @ w_in, 2, axis=-1)
    return (jax.nn.silu(gate) * up) @ w_out


def rope_tables(seq_len):
    pos = jnp.arange(seq_len, dtype=jnp.float32)
    inv_freq = ROPE_THETA ** (-jnp.arange(0, QK_DIM, 2, dtype=jnp.float32) / QK_DIM)
    ang = pos[:, None] * inv_freq[None, :]
    return jnp.cos(ang), jnp.sin(ang)


def apply_rope(t, cos, sin):
    tf = t.astype(jnp.float32)
    t1, t2 = jnp.split(tf, 2, axis=-1)
    c = cos[None, :, None, :]
    s = sin[None, :, None, :]
    out = jnp.concatenate([t1 * c - t2 * s, t2 * c + t1 * s], axis=-1)
    return out.astype(t.dtype)


def causal_depthwise_conv(x, w, b):
    seq_len = x.shape[1]
    xp = jnp.pad(x, ((0, 0), (CONV_WIDTH - 1, 0), (0, 0)))
    out = b
    for k in range(CONV_WIDTH):
        out = out + xp[:, k:k + seq_len, :] * w[k]
    return out


def _lin_rec_combine(e1, e2):
    a1, b1 = e1
    a2, b2 = e2
    return a1 * a2, a2 * b1 + b2


def rglru_block(u, w_in, b_in, conv_w, conv_b, gate_w, gate_b, lam, w_out, b_out):
    bsz, seq_len, _ = u.shape
    y = u @ w_in + b_in
    gate_branch, xb = jnp.split(y, 2, axis=-1)
    gate_branch = jax.nn.gelu(gate_branch, approximate=True)
    xb = causal_depthwise_conv(xb, conv_w, conv_b)
    xblk = xb.reshape(bsz, seq_len, LRU_BLOCKS, LRU_BLOCK_WIDTH)
    g = jnp.einsum('bsnc,ncg->bsng', xblk, gate_w) + gate_b
    g = jax.nn.sigmoid(g.astype(jnp.float32))
    gate_x = g[..., :LRU_BLOCK_WIDTH].reshape(bsz, seq_len, LRU_WIDTH)
    gate_a = g[..., LRU_BLOCK_WIDTH:].reshape(bsz, seq_len, LRU_WIDTH)
    log_a = RG_C * gate_a * jax.nn.log_sigmoid(lam.astype(jnp.float32))
    a = jnp.exp(log_a)
    mult = jnp.sqrt(-jnp.expm1(2.0 * log_a))
    b = mult * (gate_x * xb.astype(jnp.float32))
    _, h = lax.associative_scan(_lin_rec_combine, (a, b), axis=1)
    return (h.astype(u.dtype) * gate_branch) @ w_out + b_out


def shared_kv(h, kv_norm, w_kv, cos, sin):
    bsz, seq_len, _ = h.shape
    kv = (rmsnorm(h, kv_norm) @ w_kv).reshape(bsz, seq_len, N_HEADS, 2 * QK_DIM + V_DIM)
    k1 = apply_rope(kv[..., :QK_DIM], cos, sin)
    k2 = apply_rope(kv[..., QK_DIM:2 * QK_DIM], cos, sin)
    v = kv[..., 2 * QK_DIM:]
    to_bhsd = lambda t: jnp.transpose(t, (0, 2, 1, 3))
    return to_bhsd(k1), to_bhsd(k2), to_bhsd(v)


def causal_diff_attention(q1, q2, k1, k2, v, lam):
    bsz, n_heads, seq_len, _ = q1.shape
    n_blk = seq_len // Q_BLOCK
    scale = QK_DIM ** -0.5
    kpos = jnp.arange(seq_len, dtype=jnp.int32)
    neg = jnp.finfo(jnp.float32).min

    def to_blocks(q):
        q = q.reshape(bsz, n_heads, n_blk, Q_BLOCK, QK_DIM)
        return jnp.transpose(q, (2, 0, 1, 3, 4))

    def one_block(args):
        q1b, q2b, start = args
        qpos = start + jnp.arange(Q_BLOCK, dtype=jnp.int32)
        mask = kpos[None, :] <= qpos[:, None]
        s1 = jnp.einsum('bhqd,bhkd->bhqk', q1b, k1).astype(jnp.float32) * scale
        s2 = jnp.einsum('bhqd,bhkd->bhqk', q2b, k2).astype(jnp.float32) * scale
        p1 = jax.nn.softmax(jnp.where(mask, s1, neg), axis=-1)
        p2 = jax.nn.softmax(jnp.where(mask, s2, neg), axis=-1)
        p = (p1 - lam * p2).astype(v.dtype)
        return jnp.einsum('bhqk,bhkd->bhqd', p, v)

    starts = jnp.arange(n_blk, dtype=jnp.int32) * Q_BLOCK
    out = lax.map(one_block, (to_blocks(q1), to_blocks(q2), starts))
    out = jnp.transpose(out, (1, 0, 3, 2, 4))
    return out.reshape(bsz, seq_len, n_heads, V_DIM)


def diff_attn_layer(u, k1, k2, v, w_q, lam_params, subln_g, w_o, lambda_init, cos, sin):
    bsz, seq_len, _ = u.shape
    q = (u @ w_q).reshape(bsz, seq_len, N_HEADS, 2, QK_DIM)
    q1 = jnp.transpose(apply_rope(q[..., 0, :], cos, sin), (0, 2, 1, 3))
    q2 = jnp.transpose(apply_rope(q[..., 1, :], cos, sin), (0, 2, 1, 3))
    lp = lam_params.astype(jnp.float32)
    lam = jnp.exp(jnp.sum(lp[0] * lp[1])) - jnp.exp(jnp.sum(lp[2] * lp[3])) + lambda_init
    o = causal_diff_attention(q1, q2, k1, k2, v, lam)
    o = rmsnorm(o, subln_g) * (1.0 - lambda_init)
    return o.reshape(bsz, seq_len, N_HEADS * V_DIM) @ w_o


def setup_inputs(seed: int = 0) -> dict:
    key = jax.random.key(seed)
    ks = jax.random.split(key, 32)
    f32 = jnp.float32
    nrm = lambda k, shape, fan_in: jax.random.normal(k, shape, f32) * (fan_in ** -0.5)
    gain = lambda k, shape: 1.0 + 0.02 * jax.random.normal(k, shape, f32)
    small = lambda k, shape: 0.01 * jax.random.normal(k, shape, f32)
    R = LRU_WIDTH
    u = jax.random.uniform(ks[14], (N_A_LAYERS, R), f32, 0.9, 0.999)
    a0 = u ** (1.0 / RG_C)
    rg_lambda = jnp.log(a0) - jnp.log1p(-a0)
    return {
        "x": jax.random.normal(ks[0], (BATCH, SEQ, D_MODEL), f32),
        "ffn1_norm": gain(ks[1], (DEPTH, D_MODEL)),
        "ffn1_w_in": nrm(ks[2], (DEPTH, D_MODEL, 2 * D_FF), D_MODEL),
        "ffn1_w_out": nrm(ks[3], (DEPTH, D_FF, D_MODEL), D_FF),
        "mix_norm": gain(ks[4], (DEPTH, D_MODEL)),
        "ffn2_norm": gain(ks[5], (DEPTH, D_MODEL)),
        "ffn2_w_in": nrm(ks[6], (DEPTH, D_MODEL, 2 * D_FF), D_MODEL),
        "ffn2_w_out": nrm(ks[7], (DEPTH, D_FF, D_MODEL), D_FF),
        "rg_w_in": nrm(ks[8], (N_A_LAYERS, D_MODEL, 2 * R), D_MODEL),
        "rg_b_in": small(ks[9], (N_A_LAYERS, 2 * R)),
        "rg_conv_w": nrm(ks[10], (N_A_LAYERS, CONV_WIDTH, R), CONV_WIDTH),
        "rg_conv_b": small(ks[11], (N_A_LAYERS, R)),
        "rg_gate_w": nrm(ks[12], (N_A_LAYERS, LRU_BLOCKS, LRU_BLOCK_WIDTH, 2 * LRU_BLOCK_WIDTH), LRU_BLOCK_WIDTH),
        "rg_gate_b": small(ks[13], (N_A_LAYERS, LRU_BLOCKS, 2 * LRU_BLOCK_WIDTH)),
        "rg_lambda": rg_lambda,
        "rg_w_out": nrm(ks[15], (N_A_LAYERS, R, D_MODEL), R),
        "rg_b_out": small(ks[16], (N_A_LAYERS, D_MODEL)),
        "kv_norm": gain(ks[17], (D_MODEL,)),
        "w_kv": nrm(ks[18], (D_MODEL, KV_WIDTH), D_MODEL),
        "diff_w_q": nrm(ks[19], (N_B_LAYERS, D_MODEL, N_HEADS * 2 * QK_DIM), D_MODEL),
        "diff_lambda": 0.1 * jax.random.normal(ks[20], (N_B_LAYERS, 4, QK_DIM), f32),
        "diff_subln": gain(ks[21], (N_B_LAYERS, V_DIM)),
        "diff_w_o": nrm(ks[22], (N_B_LAYERS, N_HEADS * V_DIM, D_MODEL), N_HEADS * V_DIM),
        "final_norm": gain(ks[23], (D_MODEL,)),
    }


def reference(x, ffn1_norm, ffn1_w_in, ffn1_w_out, mix_norm, ffn2_norm, ffn2_w_in, ffn2_w_out,
              rg_w_in, rg_b_in, rg_conv_w, rg_conv_b, rg_gate_w, rg_gate_b, rg_lambda, rg_w_out, rg_b_out,
              kv_norm, w_kv, diff_w_q, diff_lambda, diff_subln, diff_w_o, final_norm):
    seq_len = x.shape[1]
    cos, sin = rope_tables(seq_len)
    h = x
    k1 = k2 = v = None
    for l in range(DEPTH):
        if l == N_A_LAYERS:
            k1, k2, v = shared_kv(h, kv_norm, w_kv, cos, sin)
        h = h + 0.5 * swiglu_ffn(rmsnorm(h, ffn1_norm[l]), ffn1_w_in[l], ffn1_w_out[l])
        u = rmsnorm(h, mix_norm[l])
        if l < N_A_LAYERS:
            h = h + rglru_block(u, rg_w_in[l], rg_b_in[l], rg_conv_w[l], rg_conv_b[l],
                                rg_gate_w[l], rg_gate_b[l], rg_lambda[l], rg_w_out[l], rg_b_out[l])
        else:
            j = l - N_A_LAYERS
            lambda_init = 0.8 - 0.6 * math.exp(-0.3 * l)
            h = h + diff_attn_layer(u, k1, k2, v, diff_w_q[j], diff_lambda[j], diff_subln[j],
                                    diff_w_o[j], lambda_init, cos, sin)
        h = h + 0.5 * swiglu_ffn(rmsnorm(h, ffn2_norm[l]), ffn2_w_in[l], ffn2_w_out[l])
    return rmsnorm(h, final_norm)
```

```python
import functools
import math

import jax
import jax.numpy as jnp
from jax import lax
from jax.experimental import pallas as pl
from jax.experimental.pallas import tpu as pltpu

D_MODEL = 1024
BATCH = 8
SEQ = 2048
DEPTH = 2
N_A_LAYERS = DEPTH // 2
D_FF = 2816
LRU_WIDTH = D_MODEL
LRU_BLOCKS = 4
LRU_BLOCK_WIDTH = LRU_WIDTH // LRU_BLOCKS
CONV_WIDTH = 4
RG_C = 8.0
N_HEADS = 8
QK_DIM = 64
V_DIM = 2 * QK_DIM
KV_WIDTH = N_HEADS * (2 * QK_DIM + V_DIM)
ROPE_THETA = 10000.0
EPS = 1e-6

F32 = jnp.float32
BF16 = jnp.bfloat16

V7X_LANES = 128
V7X_SUBLANES = 8
V7X_VMEM_BYTES = 64 * 1024 * 1024
V7X_MXU_DIM = 256

HEAD_QK_LANES = 2 * QK_DIM
HEAD_KV_LANES = 2 * QK_DIM + V_DIM
assert HEAD_QK_LANES == V7X_LANES and V_DIM == V7X_LANES


def _vmem_limit(est_bytes):
    return int(min(est_bytes * 5 // 4 + (8 << 20), V7X_VMEM_BYTES - (4 << 20)))


def _resident(shape):
    nd = len(shape)
    return pl.BlockSpec(shape, lambda *_: (0,) * nd, pipeline_mode=pl.Buffered(1))


def _rmsnorm(x, g):
    ms = jnp.mean(x * x, axis=-1, keepdims=True)
    return x * lax.rsqrt(ms + EPS) * g


def _dot(a, b):
    return jnp.dot(a, b, preferred_element_type=F32)


FFN_ROWS = 512
FFN_CHUNKS = (512, 512, 512, 512, 512, 256)
assert sum(FFN_CHUNKS) == D_FF and all(c % V7X_MXU_DIM == 0 for c in FFN_CHUNKS)


def _ffn_body(h_ref, g_ref, win_ref, wout_ref, *rest, final_norm):
    if final_norm:
        fg_ref, o_ref = rest
    else:
        (o_ref,) = rest
    x = h_ref[...]
    u = _rmsnorm(x, g_ref[...]).astype(BF16)
    acc = None
    off = 0
    for width in FFN_CHUNKS:
        gate = _dot(u, win_ref[:, off:off + width])
        up = _dot(u, win_ref[:, D_FF + off:D_FF + off + width])
        act = (gate * jax.nn.sigmoid(gate) * up).astype(BF16)
        part = _dot(act, wout_ref[off:off + width, :])
        acc = part if acc is None else acc + part
        off += width
    y = x + 0.5 * acc
    if final_norm:
        y = _rmsnorm(y, fg_ref[...])
    o_ref[...] = y


def _ffn(h, norm_g, w_in, w_out, final_g=None):
    rows = h.shape[0]
    tm = FFN_ROWS
    final_norm = final_g is not None
    row_spec = pl.BlockSpec((tm, D_MODEL), lambda i: (i, 0))
    in_specs = [row_spec, _resident((1, D_MODEL)), _resident((D_MODEL, 2 * D_FF)),
                _resident((D_FF, D_MODEL))]
    args = [h, norm_g.reshape(1, D_MODEL), w_in, w_out]
    if final_norm:
        in_specs.append(_resident((1, D_MODEL)))
        args.append(final_g.reshape(1, D_MODEL))
    est = (3 * D_MODEL * D_FF * 2 + 4 * tm * D_MODEL * 4
           + tm * max(FFN_CHUNKS) * 4 * 3 + tm * D_MODEL * 4 * 2)
    return pl.pallas_call(
        functools.partial(_ffn_body, final_norm=final_norm),
        grid=(rows // tm,),
        in_specs=in_specs,
        out_specs=row_spec,
        out_shape=jax.ShapeDtypeStruct((rows, D_MODEL), F32),
        compiler_params=pltpu.CompilerParams(
            dimension_semantics=("arbitrary",), vmem_limit_bytes=_vmem_limit(est)),
        name="ffn_final" if final_norm else "ffn",
    )(*args)


LRU_STEPS = 64
LRU_ROWS = LRU_STEPS * BATCH
CONV_HIST = (CONV_WIDTH - 1) * BATCH
assert BATCH == V7X_SUBLANES and LRU_ROWS >= CONV_HIST


def _rglru_body(h_ref, g_ref, win_ref, bin_ref, cw_ref, cb_ref, gw_ref, gb_ref, lam_ref,
                wout_ref, bout_ref, o_ref, y_scr, xp_scr, a_scr, b_scr, state_scr):
    tm = LRU_ROWS
    R = LRU_WIDTH
    BW = LRU_BLOCK_WIDTH

    @pl.when(pl.program_id(0) == 0)
    def _():
        xp_scr[0:CONV_HIST, :] = jnp.zeros((CONV_HIST, R), F32)
        state_scr[...] = jnp.zeros((BATCH, R), F32)

    u = _rmsnorm(h_ref[...], g_ref[...]).astype(BF16)
    y_scr[...] = _dot(u, win_ref[...]) + bin_ref[...]

    xp_scr[CONV_HIST:CONV_HIST + tm, :] = y_scr[:, R:]
    lam = lam_ref[...]
    log_sig = jnp.minimum(lam, 0.0) - jnp.log1p(jnp.exp(-jnp.abs(lam)))
    for n in range(LRU_BLOCKS):
        cols = slice(n * BW, (n + 1) * BW)
        xn = cb_ref[:, cols]
        for k in range(CONV_WIDTH):
            xn = xn + xp_scr[k * BATCH:k * BATCH + tm, cols] * cw_ref[k:k + 1, cols]
        gates = jax.nn.sigmoid(_dot(xn.astype(BF16), gw_ref[n]) + gb_ref[n])
        gate_x = gates[:, :BW]
        gate_a = gates[:, BW:]
        log_a = RG_C * gate_a * log_sig[:, cols]
        a = jnp.exp(log_a)
        mult = jnp.sqrt(-jnp.tanh(log_a) * (1.0 + a * a))
        a_scr[:, cols] = a
        b_scr[:, cols] = mult * (gate_x * xn)
    xp_scr[0:CONV_HIST, :] = xp_scr[tm:tm + CONV_HIST, :]

    def step(t, h):
        r = pl.multiple_of(t * BATCH, BATCH)
        h = a_scr[pl.ds(r, BATCH), :] * h + b_scr[pl.ds(r, BATCH), :]
        b_scr[pl.ds(r, BATCH), :] = h
        return h

    state_scr[...] = lax.fori_loop(0, LRU_STEPS, step, state_scr[...], unroll=8)

    gate_branch = jax.nn.gelu(y_scr[:, :R], approximate=True)
    out = _dot((b_scr[...] * gate_branch).astype(BF16), wout_ref[...]) + bout_ref[...]
    o_ref[...] = h_ref[...] + out


def _rglru(h_tm, norm_g, w_in, b_in, conv_w, conv_b, gate_w, gate_b, lam, w_out, b_out):
    rows = h_tm.shape[0]
    tm = LRU_ROWS
    R = LRU_WIDTH
    row_spec = pl.BlockSpec((tm, D_MODEL), lambda i: (i, 0))
    est = ((D_MODEL * 2 * R + R * D_MODEL + LRU_BLOCKS * LRU_BLOCK_WIDTH * 2 * LRU_BLOCK_WIDTH) * 2
           + 4 * tm * D_MODEL * 4 + tm * 2 * R * 4 + (tm + CONV_HIST) * R * 4 + 2 * tm * R * 4
           + 4 * tm * R * 4)
    return pl.pallas_call(
        _rglru_body,
        grid=(rows // tm,),
        in_specs=[row_spec, _resident((1, D_MODEL)), _resident((D_MODEL, 2 * R)),
                  _resident((1, 2 * R)), _resident((CONV_WIDTH, R)), _resident((1, R)),
                  _resident((LRU_BLOCKS, LRU_BLOCK_WIDTH, 2 * LRU_BLOCK_WIDTH)),
                  _resident((LRU_BLOCKS, 1, 2 * LRU_BLOCK_WIDTH)), _resident((1, R)),
                  _resident((R, D_MODEL)), _resident((1, D_MODEL))],
        out_specs=row_spec,
        out_shape=jax.ShapeDtypeStruct((rows, D_MODEL), F32),
        scratch_shapes=[pltpu.VMEM((tm, 2 * R), F32), pltpu.VMEM((tm + CONV_HIST, R), F32),
                        pltpu.VMEM((tm, R), F32), pltpu.VMEM((tm, R), F32),
                        pltpu.VMEM((BATCH, R), F32)],
        compiler_params=pltpu.CompilerParams(
            dimension_semantics=("arbitrary",), vmem_limit_bytes=_vmem_limit(est)),
        name="rglru",
    )(h_tm, norm_g.reshape(1, D_MODEL), w_in, b_in.reshape(1, 2 * R), conv_w,
      conv_b.reshape(1, R), gate_w, gate_b.reshape(LRU_BLOCKS, 1, 2 * LRU_BLOCK_WIDTH),
      lam.reshape(1, R), w_out, b_out.reshape(1, D_MODEL))


PROJ_ROWS = 512


def _rope_tables():
    pos = jnp.arange(SEQ, dtype=F32)
    inv_freq = ROPE_THETA ** (-jnp.arange(0, QK_DIM, 2, dtype=F32) / QK_DIM)
    ang = pos[:, None] * inv_freq[None, :]
    cos, sin = jnp.cos(ang), jnp.sin(ang)
    zero = jnp.zeros_like(sin)
    cos_t = jnp.concatenate([cos, cos, cos, cos], axis=-1)
    sin_lo = jnp.concatenate([-sin, zero, -sin, zero], axis=-1)
    sin_hi = jnp.concatenate([zero, sin, zero, sin], axis=-1)
    return cos_t, sin_lo, sin_hi


def _proj_rope_body(h_ref, g_ref, w_ref, cos_ref, slo_ref, shi_ref, o_ref, *, head_lanes, scale):
    u = _rmsnorm(h_ref[...], g_ref[...]).astype(BF16)
    y = _dot(u, w_ref[...])
    cos, slo, shi = cos_ref[...], slo_ref[...], shi_ref[...]
    half = QK_DIM // 2
    for base in range(0, y.shape[1], head_lanes):
        t = y[:, base:base + V7X_LANES]
        r = (t * cos + pltpu.roll(t, V7X_LANES - half, 1) * slo + pltpu.roll(t, half, 1) * shi)
        o_ref[:, base:base + V7X_LANES] = (r * scale).astype(BF16)
        if head_lanes > V7X_LANES:
            o_ref[:, base + V7X_LANES:base + head_lanes] = (
                y[:, base + V7X_LANES:base + head_lanes].astype(BF16))


def _proj_rope(h, norm_g, w, tables, head_lanes, scale):
    rows = h.shape[0]
    tm = PROJ_ROWS
    n_out = w.shape[1]
    seq_tiles = SEQ // tm
    row_spec = pl.BlockSpec((tm, D_MODEL), lambda i: (i, 0))
    tab_spec = pl.BlockSpec((tm, V7X_LANES), lambda i: (i % seq_tiles, 0))
    est = (D_MODEL * n_out * 2 + 2 * tm * D_MODEL * 4 + 2 * tm * n_out * 2 + tm * n_out * 4 * 2
           + 6 * tm * V7X_LANES * 4)
    return pl.pallas_call(
        functools.partial(_proj_rope_body, head_lanes=head_lanes, scale=scale),
        grid=(rows // tm,),
        in_specs=[row_spec, _resident((1, D_MODEL)), _resident((D_MODEL, n_out)),
                  tab_spec, tab_spec, tab_spec],
        out_specs=pl.BlockSpec((tm, n_out), lambda i: (i, 0)),
        out_shape=jax.ShapeDtypeStruct((rows, n_out), BF16),
        compiler_params=pltpu.CompilerParams(
            dimension_semantics=("arbitrary",), vmem_limit_bytes=_vmem_limit(est)),
        name="proj_rope_%d" % n_out,
    )(h, norm_g.reshape(1, D_MODEL), w, *tables)


ATT_Q = 256
ATT_K = 256
assert ATT_Q == ATT_K and SEQ % ATT_Q == 0


def _attn_body(q_ref, kv_ref, lam_ref, sg_ref, o_ref, s_scr, m_scr, l_scr, acc_scr, *, lambda_init):
    tq, tk = ATT_Q, ATT_K
    neg = jnp.finfo(F32).min
    lp = lam_ref[...]
    lam = (jnp.exp(jnp.sum(lp[0:1] * lp[1:2], axis=1, keepdims=True))
           - jnp.exp(jnp.sum(lp[2:3] * lp[3:4], axis=1, keepdims=True)) + lambda_init)
    lane = lax.broadcasted_iota(jnp.int32, (tq, V7X_LANES), 1)
    row = lax.broadcasted_iota(jnp.int32, (2 * tq, tk), 0)
    col = lax.broadcasted_iota(jnp.int32, (2 * tq, tk), 1)
    diag_mask = col <= jnp.where(row >= tq, row - tq, row)

    def scores(qq, j):
        k = kv_ref[pl.ds(pl.multiple_of(j * tk, tk), tk), 0:V7X_LANES]
        return lax.dot_general(qq, k, (((1,), (1,)), ((), ())), preferred_element_type=F32)

    def fold_max(s):
        m_scr[...] = jnp.maximum(m_scr[...], jnp.maximum(s[:, :V7X_LANES], s[:, V7X_LANES:]))

    for i in range(SEQ // tq):
        q = q_ref[i * tq:(i + 1) * tq, :]
        zero = jnp.zeros_like(q)
        qq = jnp.concatenate([jnp.where(lane < QK_DIM, q, zero),
                              jnp.where(lane >= QK_DIM, q, zero)], axis=0)
        m_scr[...] = jnp.full((2 * tq, V7X_LANES), neg, F32)

        def phase1(j, carry, qq=qq):
            s = scores(qq, j)
            s_scr[j] = s
            fold_max(s)
            return carry

        lax.fori_loop(0, i, phase1, 0)
        s = jnp.where(diag_mask, scores(qq, i), neg)
        s_scr[i] = s
        fold_max(s)
        m = jnp.max(m_scr[...], axis=-1, keepdims=True)
        m_scr[...] = jnp.broadcast_to(m, (2 * tq, V7X_LANES))
        l_scr[...] = jnp.zeros((2 * tq, V7X_LANES), F32)
        acc_scr[...] = jnp.zeros((2 * tq, V_DIM), F32)

        def phase2(j, carry):
            s = s_scr[j]
            mb = m_scr[...]
            p_lo = jnp.exp(s[:, :V7X_LANES] - mb)
            p_hi = jnp.exp(s[:, V7X_LANES:] - mb)
            l_scr[...] += p_lo + p_hi
            p = jnp.concatenate([p_lo, p_hi], axis=1).astype(BF16)
            v = kv_ref[pl.ds(pl.multiple_of(j * tk, tk), tk), V7X_LANES:HEAD_KV_LANES]
            acc_scr[...] += _dot(p, v)
            return carry

        lax.fori_loop(0, i + 1, phase2, 0)
        l = jnp.sum(l_scr[...], axis=-1, keepdims=True)
        on = acc_scr[...] / l
        o = on[:tq] - lam * on[tq:]
        o = _rmsnorm(o, sg_ref[...]) * (1.0 - lambda_init)
        o_ref[i * tq:(i + 1) * tq, :] = o.astype(BF16)


def _attention(q, kv, lam_params, subln_g, lambda_init):
    tq, tk = ATT_Q, ATT_K
    est = (2 * SEQ * HEAD_QK_LANES * 2 * 2 + 2 * SEQ * HEAD_KV_LANES * 2
           + (SEQ // tk) * 2 * tq * tk * 4 + 3 * 2 * tq * V7X_LANES * 4 + 4 * 2 * tq * tk * 4)
    return pl.pallas_call(
        functools.partial(_attn_body, lambda_init=lambda_init),
        grid=(BATCH, N_HEADS),
        in_specs=[pl.BlockSpec((None, SEQ, HEAD_QK_LANES), lambda b, h: (b, 0, h)),
                  pl.BlockSpec((None, SEQ, HEAD_KV_LANES), lambda b, h: (b, 0, h)),
                  _resident((4, QK_DIM)), _resident((1, V_DIM))],
        out_specs=pl.BlockSpec((None, SEQ, V_DIM), lambda b, h: (b, 0, h)),
        out_shape=jax.ShapeDtypeStruct((BATCH, SEQ, N_HEADS * V_DIM), BF16),
        scratch_shapes=[pltpu.VMEM((SEQ // tk, 2 * tq, tk), F32),
                        pltpu.VMEM((2 * tq, V7X_LANES), F32),
                        pltpu.VMEM((2 * tq, V7X_LANES), F32),
                        pltpu.VMEM((2 * tq, V_DIM), F32)],
        compiler_params=pltpu.CompilerParams(
            dimension_semantics=("arbitrary", "arbitrary"), vmem_limit_bytes=_vmem_limit(est)),
        name="diff_attn",
    )(q, kv, lam_params, subln_g.reshape(1, V_DIM))


def _out_proj_body(h_ref, a_ref, w_ref, o_ref):
    o_ref[...] = h_ref[...] + _dot(a_ref[...], w_ref[...])


def _out_proj(h, a, w):
    rows = h.shape[0]
    tm = PROJ_ROWS
    n_in = a.shape[1]
    row_spec = pl.BlockSpec((tm, D_MODEL), lambda i: (i, 0))
    est = n_in * D_MODEL * 2 + 4 * tm * D_MODEL * 4 + 2 * tm * n_in * 2 + tm * D_MODEL * 4
    return pl.pallas_call(
        _out_proj_body,
        grid=(rows // tm,),
        in_specs=[row_spec, pl.BlockSpec((tm, n_in), lambda i: (i, 0)), _resident((n_in, D_MODEL))],
        out_specs=row_spec,
        out_shape=jax.ShapeDtypeStruct((rows, D_MODEL), F32),
        compiler_params=pltpu.CompilerParams(
            dimension_semantics=("arbitrary",), vmem_limit_bytes=_vmem_limit(est)),
        name="out_proj",
    )(h, a, w)


def kernel(x, ffn1_norm, ffn1_w_in, ffn1_w_out, mix_norm, ffn2_norm, ffn2_w_in, ffn2_w_out,
           rg_w_in, rg_b_in, rg_conv_w, rg_conv_b, rg_gate_w, rg_gate_b, rg_lambda, rg_w_out, rg_b_out,
           kv_norm, w_kv, diff_w_q, diff_lambda, diff_subln, diff_w_o, final_norm):
    assert x.shape == (BATCH, SEQ, D_MODEL) and DEPTH == 2 and N_A_LAYERS == 1
    rows = BATCH * SEQ
    bf = lambda w: w.astype(BF16)
    tables = _rope_tables()

    h = x.reshape(rows, D_MODEL)
    h = _ffn(h, ffn1_norm[0], bf(ffn1_w_in[0]), bf(ffn1_w_out[0]))
    h_tm = jnp.transpose(h.reshape(BATCH, SEQ, D_MODEL), (1, 0, 2)).reshape(rows, D_MODEL)
    h_tm = _rglru(h_tm, mix_norm[0], bf(rg_w_in[0]), rg_b_in[0], rg_conv_w[0], rg_conv_b[0],
                  bf(rg_gate_w[0]), rg_gate_b[0], rg_lambda[0], bf(rg_w_out[0]), rg_b_out[0])
    h = jnp.transpose(h_tm.reshape(SEQ, BATCH, D_MODEL), (1, 0, 2)).reshape(rows, D_MODEL)
    h = _ffn(h, ffn2_norm[0], bf(ffn2_w_in[0]), bf(ffn2_w_out[0]))

    lambda_init = 0.8 - 0.6 * math.exp(-0.3 * 1)
    kv = _proj_rope(h, kv_norm, bf(w_kv), tables, HEAD_KV_LANES, 1.0)
    h = _ffn(h, ffn1_norm[1], bf(ffn1_w_in[1]), bf(ffn1_w_out[1]))
    q = _proj_rope(h, mix_norm[1], bf(diff_w_q[0]), tables, HEAD_QK_LANES, QK_DIM ** -0.5)
    att = _attention(q.reshape(BATCH, SEQ, N_HEADS * HEAD_QK_LANES),
                     kv.reshape(BATCH, SEQ, KV_WIDTH), diff_lambda[0], diff_subln[0], lambda_init)
    h = _out_proj(h, att.reshape(rows, N_HEADS * V_DIM), bf(diff_w_o[0]))
    h = _ffn(h, ffn2_norm[1], bf(ffn2_w_in[1]), bf(ffn2_w_out[1]), final_g=final_norm)
    return h.reshape(BATCH, SEQ, D_MODEL)
```

```python
import functools
import math

import jax
import jax.numpy as jnp
from jax import lax
from jax.experimental import pallas as pl
from jax.experimental.pallas import tpu as pltpu

D_MODEL = 1024
BATCH = 8
SEQ = 2048
DEPTH = 2
N_A_LAYERS = DEPTH // 2
D_FF = 2816
LRU_WIDTH = D_MODEL
LRU_BLOCKS = 4
LRU_BLOCK_WIDTH = LRU_WIDTH // LRU_BLOCKS
CONV_WIDTH = 4
RG_C = 8.0
N_HEADS = 8
QK_DIM = 64
V_DIM = 2 * QK_DIM
KV_WIDTH = N_HEADS * (2 * QK_DIM + V_DIM)
ROPE_THETA = 10000.0
EPS = 1e-6

F32 = jnp.float32
BF16 = jnp.bfloat16

V7X_LANES = 128
V7X_SUBLANES = 8
V7X_VMEM_BYTES = 64 * 1024 * 1024
V7X_MXU_DIM = 256

HEAD_QK_LANES = 2 * QK_DIM
HEAD_KV_LANES = 2 * QK_DIM + V_DIM
assert HEAD_QK_LANES == V7X_LANES and V_DIM == V7X_LANES


def _vmem_limit(est_bytes):
    return int(min(est_bytes * 5 // 4 + (8 << 20), V7X_VMEM_BYTES - (4 << 20)))


def _resident(shape):
    nd = len(shape)
    return pl.BlockSpec(shape, lambda *_: (0,) * nd, pipeline_mode=pl.Buffered(1))


def _rmsnorm(x, g):
    ms = jnp.mean(x * x, axis=-1, keepdims=True)
    return x * lax.rsqrt(ms + EPS) * g


def _dot(a, b):
    return jnp.dot(a, b, preferred_element_type=F32)


FFN_ROWS = 512
FFN_CHUNKS = (512, 512, 512, 512, 512, 256)
assert sum(FFN_CHUNKS) == D_FF and all(c % V7X_MXU_DIM == 0 for c in FFN_CHUNKS)


def _ffn_body(h_ref, g_ref, win_ref, wout_ref, *rest, final_norm):
    if final_norm:
        fg_ref, o_ref = rest
    else:
        (o_ref,) = rest
    x = h_ref[...]
    u = _rmsnorm(x, g_ref[...]).astype(BF16)
    acc = None
    off = 0
    for width in FFN_CHUNKS:
        gate = _dot(u, win_ref[:, off:off + width])
        up = _dot(u, win_ref[:, D_FF + off:D_FF + off + width])
        act = (gate * jax.nn.sigmoid(gate) * up).astype(BF16)
        part = _dot(act, wout_ref[off:off + width, :])
        acc = part if acc is None else acc + part
        off += width
    y = x + 0.5 * acc
    if final_norm:
        y = _rmsnorm(y, fg_ref[...])
    o_ref[...] = y


def _ffn(h, norm_g, w_in, w_out, final_g=None):
    rows = h.shape[0]
    tm = FFN_ROWS
    final_norm = final_g is not None
    row_spec = pl.BlockSpec((tm, D_MODEL), lambda i: (i, 0))
    in_specs = [row_spec, _resident((1, D_MODEL)), _resident((D_MODEL, 2 * D_FF)),
                _resident((D_FF, D_MODEL))]
    args = [h, norm_g.reshape(1, D_MODEL), w_in, w_out]
    if final_norm:
        in_specs.append(_resident((1, D_MODEL)))
        args.append(final_g.reshape(1, D_MODEL))
    est = (3 * D_MODEL * D_FF * 2 + 4 * tm * D_MODEL * 4
           + tm * max(FFN_CHUNKS) * 4 * 3 + tm * D_MODEL * 4 * 2)
    return pl.pallas_call(
        functools.partial(_ffn_body, final_norm=final_norm),
        grid=(rows // tm,),
        in_specs=in_specs,
        out_specs=row_spec,
        out_shape=jax.ShapeDtypeStruct((rows, D_MODEL), F32),
        compiler_params=pltpu.CompilerParams(
            dimension_semantics=("arbitrary",), vmem_limit_bytes=_vmem_limit(est)),
        name="ffn_final" if final_norm else "ffn",
    )(*args)


LRU_STEPS = 64
LRU_ROWS = LRU_STEPS * BATCH
CONV_HIST = (CONV_WIDTH - 1) * BATCH
assert BATCH == V7X_SUBLANES and LRU_ROWS >= CONV_HIST


def _rglru_body(h_ref, g_ref, win_ref, bin_ref, cw_ref, cb_ref, gw_ref, gb_ref, lam_ref,
                wout_ref, bout_ref, o_ref, y_scr, xp_scr, a_scr, b_scr, state_scr):
    tm = LRU_ROWS
    R = LRU_WIDTH
    BW = LRU_BLOCK_WIDTH

    @pl.when(pl.program_id(0) == 0)
    def _():
        xp_scr[0:CONV_HIST, :] = jnp.zeros((CONV_HIST, R), F32)
        state_scr[...] = jnp.zeros((BATCH, R), F32)

    u = _rmsnorm(h_ref[...], g_ref[...]).astype(BF16)
    y_scr[...] = _dot(u, win_ref[...]) + bin_ref[...]

    xp_scr[CONV_HIST:CONV_HIST + tm, :] = y_scr[:, R:]
    lam = lam_ref[...]
    log_sig = jnp.minimum(lam, 0.0) - jnp.log1p(jnp.exp(-jnp.abs(lam)))
    for n in range(LRU_BLOCKS):
        cols = slice(n * BW, (n + 1) * BW)
        xn = cb_ref[:, cols]
        for k in range(CONV_WIDTH):
            xn = xn + xp_scr[k * BATCH:k * BATCH + tm, cols] * cw_ref[k:k + 1, cols]
        gates = jax.nn.sigmoid(_dot(xn.astype(BF16), gw_ref[n]) + gb_ref[n])
        gate_x = gates[:, :BW]
        gate_a = gates[:, BW:]
        log_a = RG_C * gate_a * log_sig[:, cols]
        a = jnp.exp(log_a)
        mult = jnp.sqrt(-jnp.tanh(log_a) * (1.0 + a * a))
        a_scr[:, cols] = a
        b_scr[:, cols] = mult * (gate_x * xn)
    xp_scr[0:CONV_HIST, :] = xp_scr[tm:tm + CONV_HIST, :]

    def step(t, h):
        r = pl.multiple_of(t * BATCH, BATCH)
        h = a_scr[pl.ds(r, BATCH), :] * h + b_scr[pl.ds(r, BATCH), :]
        b_scr[pl.ds(r, BATCH), :] = h
        return h

    state_scr[...] = lax.fori_loop(0, LRU_STEPS, step, state_scr[...], unroll=8)

    gate_branch = jax.nn.gelu(y_scr[:, :R], approximate=True)
    out = _dot((b_scr[...] * gate_branch).astype(BF16), wout_ref[...]) + bout_ref[...]
    o_ref[...] = h_ref[...] + out


def _rglru(h_tm, norm_g, w_in, b_in, conv_w, conv_b, gate_w, gate_b, lam, w_out, b_out):
    rows = h_tm.shape[0]
    tm = LRU_ROWS
    R = LRU_WIDTH
    row_spec = pl.BlockSpec((tm, D_MODEL), lambda i: (i, 0))
    est = ((D_MODEL * 2 * R + R * D_MODEL + LRU_BLOCKS * LRU_BLOCK_WIDTH * 2 * LRU_BLOCK_WIDTH) * 2
           + 4 * tm * D_MODEL * 4 + tm * 2 * R * 4 + (tm + CONV_HIST) * R * 4 + 2 * tm * R * 4
           + 4 * tm * R * 4)
    return pl.pallas_call(
        _rglru_body,
        grid=(rows // tm,),
        in_specs=[row_spec, _resident((1, D_MODEL)), _resident((D_MODEL, 2 * R)),
                  _resident((1, 2 * R)), _resident((CONV_WIDTH, R)), _resident((1, R)),
                  _resident((LRU_BLOCKS, LRU_BLOCK_WIDTH, 2 * LRU_BLOCK_WIDTH)),
                  _resident((LRU_BLOCKS, 1, 2 * LRU_BLOCK_WIDTH)), _resident((1, R)),
                  _resident((R, D_MODEL)), _resident((1, D_MODEL))],
        out_specs=row_spec,
        out_shape=jax.ShapeDtypeStruct((rows, D_MODEL), F32),
        scratch_shapes=[pltpu.VMEM((tm, 2 * R), F32), pltpu.VMEM((tm + CONV_HIST, R), F32),
                        pltpu.VMEM((tm, R), F32), pltpu.VMEM((tm, R), F32),
                        pltpu.VMEM((BATCH, R), F32)],
        compiler_params=pltpu.CompilerParams(
            dimension_semantics=("arbitrary",), vmem_limit_bytes=_vmem_limit(est)),
        name="rglru",
    )(h_tm, norm_g.reshape(1, D_MODEL), w_in, b_in.reshape(1, 2 * R), conv_w,
      conv_b.reshape(1, R), gate_w, gate_b.reshape(LRU_BLOCKS, 1, 2 * LRU_BLOCK_WIDTH),
      lam.reshape(1, R), w_out, b_out.reshape(1, D_MODEL))


PROJ_ROWS = 512


def _rope_tables():
    pos = jnp.arange(SEQ, dtype=F32)
    inv_freq = ROPE_THETA ** (-jnp.arange(0, QK_DIM, 2, dtype=F32) / QK_DIM)
    ang = pos[:, None] * inv_freq[None, :]
    cos, sin = jnp.cos(ang), jnp.sin(ang)
    zero = jnp.zeros_like(sin)
    cos_t = jnp.concatenate([cos, cos, cos, cos], axis=-1)
    sin_lo = jnp.concatenate([-sin, zero, -sin, zero], axis=-1)
    sin_hi = jnp.concatenate([zero, sin, zero, sin], axis=-1)
    return cos_t, sin_lo, sin_hi


def _proj_rope_body(h_ref, g_ref, w_ref, cos_ref, slo_ref, shi_ref, o_ref, *, head_lanes, scale):
    u = _rmsnorm(h_ref[...], g_ref[...]).astype(BF16)
    y = _dot(u, w_ref[...])
    cos, slo, shi = cos_ref[...], slo_ref[...], shi_ref[...]
    half = QK_DIM // 2
    for base in range(0, y.shape[1], head_lanes):
        t = y[:, base:base + V7X_LANES]
        r = (t * cos + pltpu.roll(t, V7X_LANES - half, 1) * slo + pltpu.roll(t, half, 1) * shi)
        o_ref[:, base:base + V7X_LANES] = (r * scale).astype(BF16)
        if head_lanes > V7X_LANES:
            o_ref[:, base + V7X_LANES:base + head_lanes] = (
                y[:, base + V7X_LANES:base + head_lanes].astype(BF16))


def _proj_rope(h, norm_g, w, tables, head_lanes, scale):
    rows = h.shape[0]
    tm = PROJ_ROWS
    n_out = w.shape[1]
    seq_tiles = SEQ // tm
    row_spec = pl.BlockSpec((tm, D_MODEL), lambda i: (i, 0))
    tab_spec = pl.BlockSpec((tm, V7X_LANES), lambda i: (i % seq_tiles, 0))
    est = (D_MODEL * n_out * 2 + 2 * tm * D_MODEL * 4 + 2 * tm * n_out * 2 + tm * n_out * 4 * 2
           + 6 * tm * V7X_LANES * 4)
    return pl.pallas_call(
        functools.partial(_proj_rope_body, head_lanes=head_lanes, scale=scale),
        grid=(rows // tm,),
        in_specs=[row_spec, _resident((1, D_MODEL)), _resident((D_MODEL, n_out)),
                  tab_spec, tab_spec, tab_spec],
        out_specs=pl.BlockSpec((tm, n_out), lambda i: (i, 0)),
        out_shape=jax.ShapeDtypeStruct((rows, n_out), BF16),
        compiler_params=pltpu.CompilerParams(
            dimension_semantics=("arbitrary",), vmem_limit_bytes=_vmem_limit(est)),
        name="proj_rope_%d" % n_out,
    )(h, norm_g.reshape(1, D_MODEL), w, *tables)


ATT_Q = 256
ATT_SUM_ROWS = 16
ATT_Q_SCALE = math.log2(math.e) * QK_DIM ** -0.5
ATT_AHEAD = 2
assert SEQ % ATT_Q == 0


def _attn_body(q_ref, kv_ref, lam_ref, sg_ref, o_ref, vt_scr, *, lambda_init):
    tq = ATT_Q
    nt = (((1,), (1,)), ((), ()))
    neg = jnp.finfo(F32).min
    lp = lam_ref[...]
    lam = (jnp.exp(jnp.sum(lp[0:1] * lp[1:2], axis=1, keepdims=True))
           - jnp.exp(jnp.sum(lp[2:3] * lp[3:4], axis=1, keepdims=True)) + lambda_init)
    lane = lax.broadcasted_iota(jnp.int32, (tq, V7X_LANES), 1)
    kpos = lax.broadcasted_iota(jnp.int32, (tq, 2 * tq), 0)
    qpos = lax.broadcasted_iota(jnp.int32, (tq, 2 * tq), 1)
    diag_mask = kpos <= jnp.where(qpos >= tq, qpos - tq, qpos)

    for j in range(SEQ // tq):
        rows = slice(j * tq, (j + 1) * tq)
        vt_scr[0:V_DIM, rows] = kv_ref[rows, V7X_LANES:HEAD_KV_LANES].T
    ones_row = lax.broadcasted_iota(jnp.int32, (ATT_SUM_ROWS, SEQ), 0) == 0
    vt_scr[V_DIM:V_DIM + ATT_SUM_ROWS, :] = jnp.where(ones_row, 1.0, 0.0).astype(BF16)

    n_blocks = SEQ // tq
    pairs = [(i, j) for i in range(n_blocks) for j in range(i + 1)]

    def stacked_queries(i):
        q = q_ref[i * tq:(i + 1) * tq, :]
        zero = jnp.zeros_like(q)
        return jnp.concatenate([jnp.where(lane < QK_DIM, q, zero),
                                jnp.where(lane >= QK_DIM, q, zero)], axis=0)

    def scores(qq, i, j):
        s = lax.dot_general(kv_ref[j * tq:(j + 1) * tq, 0:V7X_LANES], qq, nt,
                            preferred_element_type=F32)
        return jnp.where(diag_mask, s, neg) if j == i else s

    qq_of = {}
    s_of = {}

    def issue_scores(n):
        if n < len(pairs):
            i_n, j_n = pairs[n]
            if i_n not in qq_of:
                qq_of.clear()
                qq_of[i_n] = stacked_queries(i_n)
            s_of[n] = scores(qq_of[i_n], i_n, j_n)

    for n in range(ATT_AHEAD):
        issue_scores(n)
    m = acc = None
    for n, (i, j) in enumerate(pairs):
        issue_scores(n + ATT_AHEAD)
        s = s_of.pop(n)
        block_max = jnp.max(s, axis=0, keepdims=True)
        m_new = block_max if j == 0 else jnp.maximum(m, block_max)
        p = jnp.exp2(s - m_new).astype(BF16)
        pv = _dot(vt_scr[:, j * tq:(j + 1) * tq], p)
        acc = pv if j == 0 else jnp.exp2(m - m_new) * acc + pv
        m = m_new
        if j == i:
            on = acc[0:V_DIM] * (1.0 / acc[V_DIM:V_DIM + 1])
            o = (on[:, :tq] - lam * on[:, tq:]).T
            o = _rmsnorm(o, sg_ref[...]) * (1.0 - lambda_init)
            o_ref[i * tq:(i + 1) * tq, :] = o.astype(BF16)


def _attention(q, kv, lam_params, subln_g, lambda_init):
    tq = ATT_Q
    est = (2 * SEQ * HEAD_QK_LANES * 2 * 2 + 2 * SEQ * HEAD_KV_LANES * 2
           + (V_DIM + ATT_SUM_ROWS) * SEQ * 2 + 32 * 2 * tq * tq * 4)
    return pl.pallas_call(
        functools.partial(_attn_body, lambda_init=lambda_init),
        grid=(BATCH, N_HEADS),
        in_specs=[pl.BlockSpec((None, SEQ, HEAD_QK_LANES), lambda b, h: (b, 0, h)),
                  pl.BlockSpec((None, SEQ, HEAD_KV_LANES), lambda b, h: (b, 0, h)),
                  _resident((4, QK_DIM)), _resident((1, V_DIM))],
        out_specs=pl.BlockSpec((None, SEQ, V_DIM), lambda b, h: (b, 0, h)),
        out_shape=jax.ShapeDtypeStruct((BATCH, SEQ, N_HEADS * V_DIM), BF16),
        scratch_shapes=[pltpu.VMEM((V_DIM + ATT_SUM_ROWS, SEQ), BF16)],
        compiler_params=pltpu.CompilerParams(
            dimension_semantics=("arbitrary", "arbitrary"), vmem_limit_bytes=_vmem_limit(est)),
        name="diff_attn",
    )(q, kv, lam_params, subln_g.reshape(1, V_DIM))


def _out_proj_body(h_ref, a_ref, w_ref, o_ref):
    o_ref[...] = h_ref[...] + _dot(a_ref[...], w_ref[...])


def _out_proj(h, a, w):
    rows = h.shape[0]
    tm = PROJ_ROWS
    n_in = a.shape[1]
    row_spec = pl.BlockSpec((tm, D_MODEL), lambda i: (i, 0))
    est = n_in * D_MODEL * 2 + 4 * tm * D_MODEL * 4 + 2 * tm * n_in * 2 + tm * D_MODEL * 4
    return pl.pallas_call(
        _out_proj_body,
        grid=(rows // tm,),
        in_specs=[row_spec, pl.BlockSpec((tm, n_in), lambda i: (i, 0)), _resident((n_in, D_MODEL))],
        out_specs=row_spec,
        out_shape=jax.ShapeDtypeStruct((rows, D_MODEL), F32),
        compiler_params=pltpu.CompilerParams(
            dimension_semantics=("arbitrary",), vmem_limit_bytes=_vmem_limit(est)),
        name="out_proj",
    )(h, a, w)


def kernel(x, ffn1_norm, ffn1_w_in, ffn1_w_out, mix_norm, ffn2_norm, ffn2_w_in, ffn2_w_out,
           rg_w_in, rg_b_in, rg_conv_w, rg_conv_b, rg_gate_w, rg_gate_b, rg_lambda, rg_w_out, rg_b_out,
           kv_norm, w_kv, diff_w_q, diff_lambda, diff_subln, diff_w_o, final_norm):
    assert x.shape == (BATCH, SEQ, D_MODEL) and DEPTH == 2 and N_A_LAYERS == 1
    rows = BATCH * SEQ
    bf = lambda w: w.astype(BF16)
    tables = _rope_tables()

    h = x.reshape(rows, D_MODEL)
    h = _ffn(h, ffn1_norm[0], bf(ffn1_w_in[0]), bf(ffn1_w_out[0]))
    h_tm = jnp.transpose(h.reshape(BATCH, SEQ, D_MODEL), (1, 0, 2)).reshape(rows, D_MODEL)
    h_tm = _rglru(h_tm, mix_norm[0], bf(rg_w_in[0]), rg_b_in[0], rg_conv_w[0], rg_conv_b[0],
                  bf(rg_gate_w[0]), rg_gate_b[0], rg_lambda[0], bf(rg_w_out[0]), rg_b_out[0])
    h = jnp.transpose(h_tm.reshape(SEQ, BATCH, D_MODEL), (1, 0, 2)).reshape(rows, D_MODEL)
    h = _ffn(h, ffn2_norm[0], bf(ffn2_w_in[0]), bf(ffn2_w_out[0]))

    lambda_init = 0.8 - 0.6 * math.exp(-0.3 * 1)
    kv = _proj_rope(h, kv_norm, bf(w_kv), tables, HEAD_KV_LANES, 1.0)
    h = _ffn(h, ffn1_norm[1], bf(ffn1_w_in[1]), bf(ffn1_w_out[1]))
    q = _proj_rope(h, mix_norm[1], bf(diff_w_q[0]), tables, HEAD_QK_LANES, ATT_Q_SCALE)
    att = _attention(q.reshape(BATCH, SEQ, N_HEADS * HEAD_QK_LANES),
                     kv.reshape(BATCH, SEQ, KV_WIDTH), diff_lambda[0], diff_subln[0], lambda_init)
    h = _out_proj(h, att.reshape(rows, N_HEADS * V_DIM), bf(diff_w_o[0]))
    h = _ffn(h, ffn2_norm[1], bf(ffn2_w_in[1]), bf(ffn2_w_out[1]), final_g=final_norm)
    return h.reshape(BATCH, SEQ, D_MODEL)
```

```python
import functools
import math

import jax
import jax.numpy as jnp
from jax import lax
from jax.experimental import pallas as pl
from jax.experimental.pallas import tpu as pltpu

D_MODEL = 1024
BATCH = 8
SEQ = 2048
DEPTH = 2
N_A_LAYERS = DEPTH // 2
D_FF = 2816
LRU_WIDTH = D_MODEL
LRU_BLOCKS = 4
LRU_BLOCK_WIDTH = LRU_WIDTH // LRU_BLOCKS
CONV_WIDTH = 4
RG_C = 8.0
N_HEADS = 8
QK_DIM = 64
V_DIM = 2 * QK_DIM
KV_WIDTH = N_HEADS * (2 * QK_DIM + V_DIM)
ROPE_THETA = 10000.0
EPS = 1e-6

F32 = jnp.float32
BF16 = jnp.bfloat16

V7X_LANES = 128
V7X_SUBLANES = 8
V7X_BF16_SUBLANES = 16
V7X_VMEM_BYTES = 64 * 1024 * 1024
V7X_MXU_DIM = 256

HEAD_QK_LANES = 2 * QK_DIM
HEAD_KV_LANES = 2 * QK_DIM + V_DIM
assert HEAD_QK_LANES == V7X_LANES and V_DIM == V7X_LANES


def _vmem_limit(est_bytes):
    return int(min(est_bytes * 5 // 4 + (8 << 20), V7X_VMEM_BYTES - (4 << 20)))


def _resident(shape):
    nd = len(shape)
    return pl.BlockSpec(shape, lambda *_: (0,) * nd, pipeline_mode=pl.Buffered(1))


def _rmsnorm(x, g):
    ms = jnp.mean(x * x, axis=-1, keepdims=True)
    return x * lax.rsqrt(ms + EPS) * g


def _dot(a, b):
    return jnp.dot(a, b, preferred_element_type=F32)


def _largest_divisor(n, limit, multiple):
    return max(d for d in range(multiple, min(n, limit) + 1, multiple) if n % d == 0)


CAST_BLOCK_BYTES = 4 << 20


def _cast_body(w_ref, o_ref):
    o_ref[...] = w_ref[...].astype(BF16)


def _to_bf16(w, layer=0):
    if w.ndim == 2:
        w = w[None]
    _, k, n = w.shape
    tk = _largest_divisor(k, max(CAST_BLOCK_BYTES // (4 * n), V7X_BF16_SUBLANES), V7X_BF16_SUBLANES)
    return pl.pallas_call(
        _cast_body,
        grid=(k // tk,),
        in_specs=[pl.BlockSpec((None, tk, n), lambda i: (layer, i, 0))],
        out_specs=pl.BlockSpec((tk, n), lambda i: (i, 0)),
        out_shape=jax.ShapeDtypeStruct((k, n), BF16),
        compiler_params=pltpu.CompilerParams(
            dimension_semantics=("arbitrary",), vmem_limit_bytes=_vmem_limit(2 * tk * n * 6)),
        name="cast_bf16",
    )(w)


def _rope_tables():
    pos = jnp.arange(SEQ, dtype=F32)
    inv_freq = ROPE_THETA ** (-jnp.arange(0, QK_DIM, 2, dtype=F32) / QK_DIM)
    ang = pos[:, None] * inv_freq[None, :]
    cos, sin = jnp.cos(ang), jnp.sin(ang)
    zero = jnp.zeros_like(sin)
    cos_t = jnp.concatenate([cos, cos, cos, cos], axis=-1)
    sin_lo = jnp.concatenate([-sin, zero, -sin, zero], axis=-1)
    sin_hi = jnp.concatenate([zero, sin, zero, sin], axis=-1)
    return cos_t, sin_lo, sin_hi


def _rope_store(y, cos, slo, shi, o_ref, head_lanes, scale):
    half = QK_DIM // 2
    for base in range(0, y.shape[1], head_lanes):
        t = y[:, base:base + V7X_LANES]
        r = (t * cos + pltpu.roll(t, V7X_LANES - half, 1) * slo + pltpu.roll(t, half, 1) * shi)
        o_ref[:, base:base + V7X_LANES] = (r * scale).astype(BF16)
        if head_lanes > V7X_LANES:
            o_ref[:, base + V7X_LANES:base + head_lanes] = (
                y[:, base + V7X_LANES:base + head_lanes].astype(BF16))


FFN_ROWS = 512
FFN_CHUNKS = (512, 512, 512, 512, 512, 256)
assert sum(FFN_CHUNKS) == D_FF and all(c % V7X_MXU_DIM == 0 for c in FFN_CHUNKS)
assert SEQ % FFN_ROWS == 0


def _ffn_body(*refs, pre_proj, post_head_lanes, post_scale, final_norm):
    refs = list(refs)
    h_ref = refs.pop(0)
    if pre_proj:
        a_ref, wpre_ref = refs.pop(0), refs.pop(0)
    g_ref, win_ref, wout_ref = refs.pop(0), refs.pop(0), refs.pop(0)
    if post_head_lanes:
        pg_ref, wpost_ref, cos_ref, slo_ref, shi_ref = (refs.pop(0) for _ in range(5))
    if final_norm:
        fg_ref = refs.pop(0)
    o_ref = refs.pop(0)

    x = h_ref[...]
    if pre_proj:
        x = x + _dot(a_ref[...], wpre_ref[...])
    u = _rmsnorm(x, g_ref[...]).astype(BF16)
    acc = None
    off = 0
    for width in FFN_CHUNKS:
        gate = _dot(u, win_ref[:, off:off + width])
        up = _dot(u, win_ref[:, D_FF + off:D_FF + off + width])
        act = (gate * jax.nn.sigmoid(gate) * up).astype(BF16)
        part = _dot(act, wout_ref[off:off + width, :])
        acc = part if acc is None else acc + part
        off += width
    y = x + 0.5 * acc
    if post_head_lanes:
        (op_ref,) = refs
        t = _dot(_rmsnorm(y, pg_ref[...]).astype(BF16), wpost_ref[...])
        _rope_store(t, cos_ref[...], slo_ref[...], shi_ref[...], op_ref, post_head_lanes, post_scale)
    if final_norm:
        y = _rmsnorm(y, fg_ref[...])
    o_ref[...] = y


def _ffn(h, norm_g, w_in, w_out, *, pre=None, post=None, final_g=None):
    rows = h.shape[0]
    tm = FFN_ROWS
    row_spec = pl.BlockSpec((tm, D_MODEL), lambda i: (i, 0))
    vec_spec = _resident((1, D_MODEL))
    in_specs, args = [row_spec], [h]
    est = 3 * D_MODEL * D_FF * 2 + 6 * tm * D_MODEL * 4 + tm * max(FFN_CHUNKS) * 4 * 3
    if pre is not None:
        a, w_pre = pre
        n_pre = a.shape[1]
        in_specs += [pl.BlockSpec((tm, n_pre), lambda i: (i, 0)), _resident((n_pre, D_MODEL))]
        args += [a, w_pre]
        est += n_pre * D_MODEL * 2 + 2 * tm * n_pre * 2
    in_specs += [vec_spec, _resident((D_MODEL, 2 * D_FF)), _resident((D_FF, D_MODEL))]
    args += [norm_g.reshape(1, D_MODEL), w_in, w_out]
    out_specs, out_shape = [row_spec], [jax.ShapeDtypeStruct((rows, D_MODEL), F32)]
    post_head_lanes, post_scale = 0, 1.0
    if post is not None:
        post_g, w_post, tables, post_head_lanes, post_scale = post
        n_post = w_post.shape[1]
        seq_tiles = SEQ // tm
        tab_spec = pl.BlockSpec((tm, V7X_LANES), lambda i: (i % seq_tiles, 0))
        in_specs += [vec_spec, _resident((D_MODEL, n_post)), tab_spec, tab_spec, tab_spec]
        args += [post_g.reshape(1, D_MODEL), w_post, *tables]
        out_specs.append(pl.BlockSpec((tm, n_post), lambda i: (i, 0)))
        out_shape.append(jax.ShapeDtypeStruct((rows, n_post), BF16))
        est += D_MODEL * n_post * 2 + 2 * tm * n_post * 2 + tm * n_post * 4 + 6 * tm * V7X_LANES * 4
    if final_g is not None:
        in_specs.append(vec_spec)
        args.append(final_g.reshape(1, D_MODEL))
    out = pl.pallas_call(
        functools.partial(_ffn_body, pre_proj=pre is not None, post_head_lanes=post_head_lanes,
                          post_scale=post_scale, final_norm=final_g is not None),
        grid=(rows // tm,),
        in_specs=in_specs,
        out_specs=out_specs,
        out_shape=out_shape,
        compiler_params=pltpu.CompilerParams(
            dimension_semantics=("arbitrary",), vmem_limit_bytes=_vmem_limit(est)),
        name="ffn",
    )(*args)
    return out if post is not None else out[0]


LRU_STEPS = 64
LRU_ROWS = LRU_STEPS * BATCH
CONV_HIST = (CONV_WIDTH - 1) * BATCH
assert BATCH == V7X_SUBLANES and LRU_ROWS >= CONV_HIST


def _rglru_body(h_ref, g_ref, win_ref, bin_ref, cw_ref, cb_ref, gw_ref, gb_ref, lam_ref,
                wout_ref, bout_ref, o_ref, hist_scr, state_scr):
    tm = LRU_ROWS
    R = LRU_WIDTH
    BW = LRU_BLOCK_WIDTH

    @pl.when(pl.program_id(0) == 0)
    def _():
        hist_scr[...] = jnp.zeros((CONV_HIST, R), F32)
        state_scr[...] = jnp.zeros((BATCH, R), F32)

    x = h_ref[...]
    u = _rmsnorm(x, g_ref[...]).astype(BF16)
    lam = lam_ref[...]
    log_sig = jnp.minimum(lam, 0.0) - jnp.log1p(jnp.exp(-jnp.abs(lam)))
    c_exp = (RG_C * math.log2(math.e)) * log_sig
    c_tanh = -RG_C * log_sig

    def block_cols(n, base=0):
        return slice(base + n * BW, base + (n + 1) * BW)

    def in_proj(n):
        y_gate = _dot(u, win_ref[:, block_cols(n)]) + bin_ref[:, block_cols(n)]
        y_x = _dot(u, win_ref[:, block_cols(n, R)]) + bin_ref[:, block_cols(n, R)]
        return y_gate, y_x

    def conv_gates(n, y_x):
        cols = block_cols(n)
        x_hist = jnp.concatenate([hist_scr[:, cols], y_x], axis=0)
        hist_scr[:, cols] = y_x[tm - CONV_HIST:, :]
        xn = cb_ref[:, cols]
        for k in range(CONV_WIDTH):
            xn = xn + x_hist[k * BATCH:k * BATCH + tm, :] * cw_ref[k:k + 1, cols]
        gates = jax.nn.sigmoid(_dot(xn.astype(BF16), gw_ref[n]) + gb_ref[n])
        return xn, gates

    def recur_out(n, y_gate, xn, gates):
        cols = block_cols(n)
        gate_a = gates[:, BW:]
        a = jnp.exp2(gate_a * c_exp[:, cols])
        z = jnp.tanh(gate_a * c_tanh[:, cols]) * (1.0 + a * a)
        b = jnp.where(z > 0.0, z * lax.rsqrt(z), 0.0) * (gates[:, :BW] * xn)
        h = state_scr[:, cols]
        hs = []
        for t in range(LRU_STEPS):
            r = slice(t * BATCH, (t + 1) * BATCH)
            h = a[r, :] * h + b[r, :]
            hs.append(h)
        state_scr[:, cols] = h
        c1 = math.sqrt(2.0 / math.pi)
        t = jnp.tanh(y_gate * (c1 + (c1 * 0.044715) * (y_gate * y_gate)))
        w = (0.5 * y_gate) * jnp.concatenate(hs, axis=0)
        return _dot((w + w * t).astype(BF16), wout_ref[cols, :])

    ys, cg = {}, {}
    out = None
    for s in range(LRU_BLOCKS + 2):
        if s < LRU_BLOCKS:
            ys[s] = in_proj(s)
        if 0 <= s - 1 < LRU_BLOCKS:
            cg[s - 1] = conv_gates(s - 1, ys[s - 1][1])
        if 0 <= s - 2 < LRU_BLOCKS:
            part = recur_out(s - 2, ys.pop(s - 2)[0], *cg.pop(s - 2))
            out = part if out is None else out + part
    o_ref[...] = x + (out + bout_ref[...])


def _rglru(h_tm, norm_g, w_in, b_in, conv_w, conv_b, gate_w, gate_b, lam, w_out, b_out):
    rows = h_tm.shape[0]
    tm = LRU_ROWS
    R = LRU_WIDTH
    BW = LRU_BLOCK_WIDTH
    row_spec = pl.BlockSpec((tm, D_MODEL), lambda i: (i, 0))
    est = ((D_MODEL * 2 * R + R * D_MODEL + LRU_BLOCKS * BW * 2 * BW) * 2
           + 6 * tm * D_MODEL * 4 + 16 * tm * BW * 4)
    return pl.pallas_call(
        _rglru_body,
        grid=(rows // tm,),
        in_specs=[row_spec, _resident((1, D_MODEL)), _resident((D_MODEL, 2 * R)),
                  _resident((1, 2 * R)), _resident((CONV_WIDTH, R)), _resident((1, R)),
                  _resident((LRU_BLOCKS, BW, 2 * BW)), _resident((LRU_BLOCKS, 1, 2 * BW)),
                  _resident((1, R)), _resident((R, D_MODEL)), _resident((1, D_MODEL))],
        out_specs=row_spec,
        out_shape=jax.ShapeDtypeStruct((rows, D_MODEL), F32),
        scratch_shapes=[pltpu.VMEM((CONV_HIST, R), F32), pltpu.VMEM((BATCH, R), F32)],
        compiler_params=pltpu.CompilerParams(
            dimension_semantics=("arbitrary",), vmem_limit_bytes=_vmem_limit(est)),
        name="rglru",
    )(h_tm, norm_g.reshape(1, D_MODEL), w_in, b_in.reshape(1, 2 * R), conv_w,
      conv_b.reshape(1, R), gate_w, gate_b.reshape(LRU_BLOCKS, 1, 2 * BW),
      lam.reshape(1, R), w_out, b_out.reshape(1, D_MODEL))


ATT_Q = 256
ATT_SUM_ROWS = V7X_BF16_SUBLANES
ATT_Q_SCALE = math.log2(math.e) * QK_DIM ** -0.5
ATT_AHEAD = 2
assert SEQ % ATT_Q == 0


def _attn_body(q_ref, kv_ref, lam_ref, sg_ref, o_ref, vt_scr, *, lambda_init):
    tq = ATT_Q
    nt = (((1,), (1,)), ((), ()))
    neg = jnp.finfo(F32).min
    lp = lam_ref[...]
    lam = (jnp.exp(jnp.sum(lp[0:1] * lp[1:2], axis=1, keepdims=True))
           - jnp.exp(jnp.sum(lp[2:3] * lp[3:4], axis=1, keepdims=True)) + lambda_init)
    lane = lax.broadcasted_iota(jnp.int32, (tq, V7X_LANES), 1)
    kpos = lax.broadcasted_iota(jnp.int32, (tq, 2 * tq), 0)
    qpos = lax.broadcasted_iota(jnp.int32, (tq, 2 * tq), 1)
    diag_mask = kpos <= jnp.where(qpos >= tq, qpos - tq, qpos)

    for j in range(SEQ // tq):
        rows = slice(j * tq, (j + 1) * tq)
        vt_scr[0:V_DIM, rows] = kv_ref[rows, V7X_LANES:HEAD_KV_LANES].T
    ones_row = lax.broadcasted_iota(jnp.int32, (ATT_SUM_ROWS, SEQ), 0) == 0
    vt_scr[V_DIM:V_DIM + ATT_SUM_ROWS, :] = jnp.where(ones_row, 1.0, 0.0).astype(BF16)

    n_blocks = SEQ // tq
    pairs = [(i, j) for i in range(n_blocks) for j in range(i + 1)]

    def stacked_queries(i):
        q = q_ref[i * tq:(i + 1) * tq, :]
        zero = jnp.zeros_like(q)
        return jnp.concatenate([jnp.where(lane < QK_DIM, q, zero),
                                jnp.where(lane >= QK_DIM, q, zero)], axis=0)

    def scores(qq, i, j):
        s = lax.dot_general(kv_ref[j * tq:(j + 1) * tq, 0:V7X_LANES], qq, nt,
                            preferred_element_type=F32)
        return jnp.where(diag_mask, s, neg) if j == i else s

    qq_of = {}
    s_of = {}

    def issue_scores(n):
        if n < len(pairs):
            i_n, j_n = pairs[n]
            if i_n not in qq_of:
                qq_of.clear()
                qq_of[i_n] = stacked_queries(i_n)
            s_of[n] = scores(qq_of[i_n], i_n, j_n)

    for n in range(ATT_AHEAD):
        issue_scores(n)
    m = acc = None
    for n, (i, j) in enumerate(pairs):
        issue_scores(n + ATT_AHEAD)
        s = s_of.pop(n)
        block_max = jnp.max(s, axis=0, keepdims=True)
        m_new = block_max if j == 0 else jnp.maximum(m, block_max)
        p = jnp.exp2(s - m_new).astype(BF16)
        pv = _dot(vt_scr[:, j * tq:(j + 1) * tq], p)
        acc = pv if j == 0 else jnp.exp2(m - m_new) * acc + pv
        m = m_new
        if j == i:
            on = acc[0:V_DIM] * (1.0 / acc[V_DIM:V_DIM + 1])
            o = (on[:, :tq] - lam * on[:, tq:]).T
            o = _rmsnorm(o, sg_ref[...]) * (1.0 - lambda_init)
            o_ref[i * tq:(i + 1) * tq, :] = o.astype(BF16)


def _attention(q, kv, lam_params, subln_g, lambda_init):
    tq = ATT_Q
    est = (2 * SEQ * HEAD_QK_LANES * 2 * 2 + 2 * SEQ * HEAD_KV_LANES * 2
           + (V_DIM + ATT_SUM_ROWS) * SEQ * 2 + 32 * 2 * tq * tq * 4)
    return pl.pallas_call(
        functools.partial(_attn_body, lambda_init=lambda_init),
        grid=(BATCH, N_HEADS),
        in_specs=[pl.BlockSpec((None, SEQ, HEAD_QK_LANES), lambda b, h: (b, 0, h)),
                  pl.BlockSpec((None, SEQ, HEAD_KV_LANES), lambda b, h: (b, 0, h)),
                  _resident((4, QK_DIM)), _resident((1, V_DIM))],
        out_specs=pl.BlockSpec((None, SEQ, V_DIM), lambda b, h: (b, 0, h)),
        out_shape=jax.ShapeDtypeStruct((BATCH, SEQ, N_HEADS * V_DIM), BF16),
        scratch_shapes=[pltpu.VMEM((V_DIM + ATT_SUM_ROWS, SEQ), BF16)],
        compiler_params=pltpu.CompilerParams(
            dimension_semantics=("arbitrary", "arbitrary"), vmem_limit_bytes=_vmem_limit(est)),
        name="diff_attn",
    )(q, kv, lam_params, subln_g.reshape(1, V_DIM))


def kernel(x, ffn1_norm, ffn1_w_in, ffn1_w_out, mix_norm, ffn2_norm, ffn2_w_in, ffn2_w_out,
           rg_w_in, rg_b_in, rg_conv_w, rg_conv_b, rg_gate_w, rg_gate_b, rg_lambda, rg_w_out, rg_b_out,
           kv_norm, w_kv, diff_w_q, diff_lambda, diff_subln, diff_w_o, final_norm):
    assert x.shape == (BATCH, SEQ, D_MODEL) and DEPTH == 2 and N_A_LAYERS == 1
    rows = BATCH * SEQ
    tables = _rope_tables()
    gate_w = _to_bf16(rg_gate_w.reshape(N_A_LAYERS, LRU_WIDTH, 2 * LRU_BLOCK_WIDTH)).reshape(
        LRU_BLOCKS, LRU_BLOCK_WIDTH, 2 * LRU_BLOCK_WIDTH)

    h = x.reshape(rows, D_MODEL)
    h = _ffn(h, ffn1_norm[0], _to_bf16(ffn1_w_in, 0), _to_bf16(ffn1_w_out, 0))
    h_tm = jnp.transpose(h.reshape(BATCH, SEQ, D_MODEL), (1, 0, 2)).reshape(rows, D_MODEL)
    h_tm = _rglru(h_tm, mix_norm[0], _to_bf16(rg_w_in), rg_b_in[0], rg_conv_w[0], rg_conv_b[0],
                  gate_w, rg_gate_b[0], rg_lambda[0], _to_bf16(rg_w_out), rg_b_out[0])
    h = jnp.transpose(h_tm.reshape(SEQ, BATCH, D_MODEL), (1, 0, 2)).reshape(rows, D_MODEL)
    h, kv = _ffn(h, ffn2_norm[0], _to_bf16(ffn2_w_in, 0), _to_bf16(ffn2_w_out, 0),
                 post=(kv_norm, _to_bf16(w_kv), tables, HEAD_KV_LANES, 1.0))

    lambda_init = 0.8 - 0.6 * math.exp(-0.3 * 1)
    h, q = _ffn(h, ffn1_norm[1], _to_bf16(ffn1_w_in, 1), _to_bf16(ffn1_w_out, 1),
                post=(mix_norm[1], _to_bf16(diff_w_q), tables, HEAD_QK_LANES, ATT_Q_SCALE))
    att = _attention(q.reshape(BATCH, SEQ, N_HEADS * HEAD_QK_LANES),
                     kv.reshape(BATCH, SEQ, KV_WIDTH), diff_lambda[0], diff_subln[0], lambda_init)
    h = _ffn(h, ffn2_norm[1], _to_bf16(ffn2_w_in, 1), _to_bf16(ffn2_w_out, 1),
             pre=(att.reshape(rows, N_HEADS * V_DIM), _to_bf16(diff_w_o)), final_g=final_norm)
    return h.reshape(BATCH, SEQ, D_MODEL)
```

```python
import functools
import math

import jax
import jax.numpy as jnp
from jax import lax
from jax.experimental import pallas as pl
from jax.experimental.pallas import tpu as pltpu

D_MODEL = 1024
BATCH = 8
SEQ = 2048
DEPTH = 2
N_A_LAYERS = DEPTH // 2
D_FF = 2816
LRU_WIDTH = D_MODEL
LRU_BLOCKS = 4
LRU_BLOCK_WIDTH = LRU_WIDTH // LRU_BLOCKS
CONV_WIDTH = 4
RG_C = 8.0
N_HEADS = 8
QK_DIM = 64
V_DIM = 2 * QK_DIM
KV_WIDTH = N_HEADS * (2 * QK_DIM + V_DIM)
ROPE_THETA = 10000.0
EPS = 1e-6

F32 = jnp.float32
BF16 = jnp.bfloat16

V7X_LANES = 128
V7X_SUBLANES = 8
V7X_BF16_SUBLANES = 16
V7X_VMEM_BYTES = 64 * 1024 * 1024
V7X_MXU_DIM = 256

HEAD_QK_LANES = 2 * QK_DIM
HEAD_KV_LANES = 2 * QK_DIM + V_DIM
assert HEAD_QK_LANES == V7X_LANES and V_DIM == V7X_LANES

W_CHUNK = V7X_MXU_DIM


def _vmem_limit(est_bytes):
    return int(min(est_bytes * 5 // 4 + (8 << 20), V7X_VMEM_BYTES - (4 << 20)))


def _resident(shape):
    nd = len(shape)
    return pl.BlockSpec(shape, lambda *_: (0,) * nd, pipeline_mode=pl.Buffered(1))


def _col_chunks(layer, k, n_chunks):
    return pl.BlockSpec((None, k, W_CHUNK), lambda s: (layer, 0, jnp.minimum(s, n_chunks - 1)))


def _row_chunks(layer, n, n_chunks):
    return pl.BlockSpec((None, W_CHUNK, n), lambda s: (layer, jnp.minimum(s, n_chunks - 1), 0))


def _cast_chunk(step, n_chunks, w_ref, w_scr):
    @pl.when(step < n_chunks)
    def _():
        w_scr[step] = w_ref[...].astype(BF16)


def _rmsnorm(x, g):
    ms = jnp.mean(x * x, axis=-1, keepdims=True)
    return x * lax.rsqrt(ms + EPS) * g


def _dot(a, b):
    return jnp.dot(a, b, preferred_element_type=F32)


def _rope_tables():
    pos = jnp.arange(SEQ, dtype=F32)
    inv_freq = ROPE_THETA ** (-jnp.arange(0, QK_DIM, 2, dtype=F32) / QK_DIM)
    ang = pos[:, None] * inv_freq[None, :]
    cos, sin = jnp.cos(ang), jnp.sin(ang)
    zero = jnp.zeros_like(sin)
    cos_t = jnp.concatenate([cos, cos, cos, cos], axis=-1)
    sin_lo = jnp.concatenate([-sin, zero, -sin, zero], axis=-1)
    sin_hi = jnp.concatenate([zero, sin, zero, sin], axis=-1)
    return cos_t, sin_lo, sin_hi


def _rope_store(y, cos, slo, shi, o_ref, col0, head_lanes, scale):
    half = QK_DIM // 2
    for base in range(0, y.shape[1], head_lanes):
        t = y[:, base:base + V7X_LANES]
        r = (t * cos + pltpu.roll(t, V7X_LANES - half, 1) * slo + pltpu.roll(t, half, 1) * shi)
        o_ref[:, col0 + base:col0 + base + V7X_LANES] = (r * scale).astype(BF16)
        if head_lanes > V7X_LANES:
            o_ref[:, col0 + base + V7X_LANES:col0 + base + head_lanes] = (
                y[:, base + V7X_LANES:base + head_lanes].astype(BF16))


FFN_ROWS = 512
FFN_IN_CHUNKS = 2 * D_FF // W_CHUNK
FFN_OUT_CHUNKS = D_FF // W_CHUNK
FFN_PROLOGUE = FFN_IN_CHUNKS
assert D_FF % W_CHUNK == 0 and SEQ % FFN_ROWS == 0 and D_MODEL % W_CHUNK == 0


def _ffn_body(*refs, pre_chunks, post_chunks, post_head_lanes, post_scale, final_norm):
    refs = list(refs)
    take = lambda k: [refs.pop(0) for _ in range(k)]
    (h_ref,) = take(1)
    if pre_chunks:
        a_ref, wpre_ref = take(2)
    g_ref, win_ref, wout_ref = take(3)
    if post_chunks:
        pg_ref, wpost_ref, cos_ref, slo_ref, shi_ref = take(5)
    if final_norm:
        (fg_ref,) = take(1)
    (o_ref,) = take(1)
    if post_chunks:
        (op_ref,) = take(1)
    win_scr, wout_scr = take(2)
    if pre_chunks:
        (wpre_scr,) = take(1)
    if post_chunks:
        (wpost_scr,) = take(1)

    step = pl.program_id(0)
    _cast_chunk(step, FFN_IN_CHUNKS, win_ref, win_scr)
    _cast_chunk(step, FFN_OUT_CHUNKS, wout_ref, wout_scr)
    if pre_chunks:
        _cast_chunk(step, pre_chunks, wpre_ref, wpre_scr)
    if post_chunks:
        _cast_chunk(step, post_chunks, wpost_ref, wpost_scr)

    @pl.when(step >= FFN_PROLOGUE)
    def _():
        x = h_ref[...]
        if pre_chunks:
            for c in range(pre_chunks):
                x = x + _dot(a_ref[:, c * W_CHUNK:(c + 1) * W_CHUNK], wpre_scr[c])
        u = _rmsnorm(x, g_ref[...]).astype(BF16)
        acc = None
        for c in range(FFN_OUT_CHUNKS):
            gate = _dot(u, win_scr[c])
            up = _dot(u, win_scr[FFN_OUT_CHUNKS + c])
            act = (gate * jax.nn.sigmoid(gate) * up).astype(BF16)
            part = _dot(act, wout_scr[c])
            acc = part if acc is None else acc + part
        y = x + 0.5 * acc
        if post_chunks:
            un = _rmsnorm(y, pg_ref[...]).astype(BF16)
            cos, slo, shi = cos_ref[...], slo_ref[...], shi_ref[...]
            for c in range(post_chunks):
                _rope_store(_dot(un, wpost_scr[c]), cos, slo, shi, op_ref, c * W_CHUNK,
                            post_head_lanes, post_scale)
        o_ref[...] = _rmsnorm(y, fg_ref[...]) if final_norm else y


def _ffn(h, layer, norm_g, w_in, w_out, *, pre=None, post=None, final_g=None):
    rows = h.shape[0]
    tm = FFN_ROWS
    tile = lambda s: jnp.maximum(s - FFN_PROLOGUE, 0)
    row_spec = pl.BlockSpec((tm, D_MODEL), lambda s: (tile(s), 0))
    vec_spec = _resident((1, D_MODEL))
    in_specs, args = [row_spec], [h]
    scratch = [pltpu.VMEM((FFN_IN_CHUNKS, D_MODEL, W_CHUNK), BF16),
               pltpu.VMEM((FFN_OUT_CHUNKS, W_CHUNK, D_MODEL), BF16)]
    est = (3 * D_MODEL * D_FF * 2 + 4 * D_MODEL * W_CHUNK * 4 + 6 * tm * D_MODEL * 4
           + 6 * tm * W_CHUNK * 4)
    pre_chunks = post_chunks = post_head_lanes = 0
    post_scale = 1.0
    if pre is not None:
        a, w_pre = pre
        n_pre = a.shape[1]
        pre_chunks = n_pre // W_CHUNK
        in_specs += [pl.BlockSpec((tm, n_pre), lambda s: (tile(s), 0)),
                     _row_chunks(0, D_MODEL, pre_chunks)]
        args += [a, w_pre]
        est += n_pre * D_MODEL * 2 + 2 * tm * n_pre * 2 + 2 * W_CHUNK * D_MODEL * 4
    in_specs += [vec_spec, _col_chunks(layer, D_MODEL, FFN_IN_CHUNKS),
                 _row_chunks(layer, D_MODEL, FFN_OUT_CHUNKS)]
    args += [norm_g.reshape(1, D_MODEL), w_in, w_out]
    out_specs, out_shape = [row_spec], [jax.ShapeDtypeStruct((rows, D_MODEL), F32)]
    if post is not None:
        post_g, w_post, tables, post_head_lanes, post_scale = post
        n_post = w_post.shape[2]
        post_chunks = n_post // W_CHUNK
        seq_tiles = SEQ // tm
        tab_spec = pl.BlockSpec((tm, V7X_LANES), lambda s: (tile(s) % seq_tiles, 0))
        in_specs += [vec_spec, _col_chunks(0, D_MODEL, post_chunks), tab_spec, tab_spec, tab_spec]
        args += [post_g.reshape(1, D_MODEL), w_post, *tables]
        out_specs.append(pl.BlockSpec((tm, n_post), lambda s: (tile(s), 0)))
        out_shape.append(jax.ShapeDtypeStruct((rows, n_post), BF16))
        est += (D_MODEL * n_post * 2 + 2 * tm * n_post * 2 + 2 * D_MODEL * W_CHUNK * 4
                + 6 * tm * V7X_LANES * 4)
    if final_g is not None:
        in_specs.append(vec_spec)
        args.append(final_g.reshape(1, D_MODEL))
    if pre is not None:
        scratch.append(pltpu.VMEM((pre_chunks, W_CHUNK, D_MODEL), BF16))
    if post is not None:
        scratch.append(pltpu.VMEM((post_chunks, D_MODEL, W_CHUNK), BF16))
    assert max(pre_chunks, post_chunks, FFN_OUT_CHUNKS) <= FFN_PROLOGUE
    out = pl.pallas_call(
        functools.partial(_ffn_body, pre_chunks=pre_chunks, post_chunks=post_chunks,
                          post_head_lanes=post_head_lanes, post_scale=post_scale,
                          final_norm=final_g is not None),
        grid=(FFN_PROLOGUE + rows // tm,),
        in_specs=in_specs,
        out_specs=out_specs,
        out_shape=out_shape,
        scratch_shapes=scratch,
        compiler_params=pltpu.CompilerParams(
            dimension_semantics=("arbitrary",), vmem_limit_bytes=_vmem_limit(est)),
        name="ffn",
    )(*args)
    return out if post is not None else out[0]


LRU_STEPS = 64
LRU_ROWS = LRU_STEPS * BATCH
CONV_HIST = (CONV_WIDTH - 1) * BATCH
LRU_IN_CHUNKS = 2 * LRU_WIDTH // W_CHUNK
LRU_PROLOGUE = LRU_IN_CHUNKS
assert BATCH == V7X_SUBLANES and LRU_ROWS >= CONV_HIST and LRU_BLOCK_WIDTH == W_CHUNK


def _rglru_body(h_ref, g_ref, win_ref, bin_ref, cw_ref, cb_ref, gw_ref, gb_ref, lam_ref,
                wout_ref, bout_ref, o_ref, win_scr, gw_scr, wout_scr, hist_scr, state_scr):
    tm = LRU_ROWS
    R = LRU_WIDTH
    BW = LRU_BLOCK_WIDTH

    step = pl.program_id(0)
    _cast_chunk(step, LRU_IN_CHUNKS, win_ref, win_scr)
    _cast_chunk(step, LRU_BLOCKS, gw_ref, gw_scr)
    _cast_chunk(step, LRU_BLOCKS, wout_ref, wout_scr)

    @pl.when(step == LRU_PROLOGUE)
    def _():
        hist_scr[...] = jnp.zeros((CONV_HIST, R), F32)
        state_scr[...] = jnp.zeros((BATCH, R), F32)

    def block_cols(n, base=0):
        return slice(base + n * BW, base + (n + 1) * BW)

    @pl.when(step >= LRU_PROLOGUE)
    def _():
        x = h_ref[...]
        u = _rmsnorm(x, g_ref[...]).astype(BF16)
        lam = lam_ref[...]
        log_sig = jnp.minimum(lam, 0.0) - jnp.log1p(jnp.exp(-jnp.abs(lam)))
        c_exp = (RG_C * math.log2(math.e)) * log_sig
        c_tanh = -RG_C * log_sig

        def in_proj(n):
            y_gate = _dot(u, win_scr[n]) + bin_ref[:, block_cols(n)]
            y_x = _dot(u, win_scr[LRU_BLOCKS + n]) + bin_ref[:, block_cols(n, R)]
            return y_gate, y_x

        def conv_gates(n, y_x):
            cols = block_cols(n)
            x_hist = jnp.concatenate([hist_scr[:, cols], y_x], axis=0)
            hist_scr[:, cols] = y_x[tm - CONV_HIST:, :]
            xn = cb_ref[:, cols]
            for k in range(CONV_WIDTH):
                xn = xn + x_hist[k * BATCH:k * BATCH + tm, :] * cw_ref[k:k + 1, cols]
            gates = jax.nn.sigmoid(_dot(xn.astype(BF16), gw_scr[n]) + gb_ref[n])
            return xn, gates

        def recur_out(n, y_gate, xn, gates):
            cols = block_cols(n)
            gate_a = gates[:, BW:]
            a = jnp.exp2(gate_a * c_exp[:, cols])
            z = jnp.tanh(gate_a * c_tanh[:, cols]) * (1.0 + a * a)
            b = jnp.where(z > 0.0, z * lax.rsqrt(z), 0.0) * (gates[:, :BW] * xn)
            h = state_scr[:, cols]
            hs = []
            for t in range(LRU_STEPS):
                r = slice(t * BATCH, (t + 1) * BATCH)
                h = a[r, :] * h + b[r, :]
                hs.append(h)
            state_scr[:, cols] = h
            c1 = math.sqrt(2.0 / math.pi)
            t = jnp.tanh(y_gate * (c1 + (c1 * 0.044715) * (y_gate * y_gate)))
            w = (0.5 * y_gate) * jnp.concatenate(hs, axis=0)
            return _dot((w + w * t).astype(BF16), wout_scr[n])

        ys, cg = {}, {}
        out = None
        for s in range(LRU_BLOCKS + 2):
            if s < LRU_BLOCKS:
                ys[s] = in_proj(s)
            if 0 <= s - 1 < LRU_BLOCKS:
                cg[s - 1] = conv_gates(s - 1, ys[s - 1][1])
            if 0 <= s - 2 < LRU_BLOCKS:
                part = recur_out(s - 2, ys.pop(s - 2)[0], *cg.pop(s - 2))
                out = part if out is None else out + part
        o_ref[...] = x + (out + bout_ref[...])


def _rglru(h_tm, norm_g, w_in, b_in, conv_w, conv_b, gate_w, gate_b, lam, w_out, b_out):
    rows = h_tm.shape[0]
    tm = LRU_ROWS
    R = LRU_WIDTH
    BW = LRU_BLOCK_WIDTH
    row_spec = pl.BlockSpec((tm, D_MODEL), lambda s: (jnp.maximum(s - LRU_PROLOGUE, 0), 0))
    gate_spec = pl.BlockSpec((None, None, BW, 2 * BW),
                             lambda s: (0, jnp.minimum(s, LRU_BLOCKS - 1), 0, 0))
    est = ((D_MODEL * 2 * R + R * D_MODEL + LRU_BLOCKS * BW * 2 * BW) * 2
           + 2 * (D_MODEL * W_CHUNK + BW * 2 * BW + W_CHUNK * D_MODEL) * 4
           + 6 * tm * D_MODEL * 4 + 16 * tm * BW * 4)
    return pl.pallas_call(
        _rglru_body,
        grid=(LRU_PROLOGUE + rows // tm,),
        in_specs=[row_spec, _resident((1, D_MODEL)), _col_chunks(0, D_MODEL, LRU_IN_CHUNKS),
                  _resident((1, 2 * R)), _resident((CONV_WIDTH, R)), _resident((1, R)),
                  gate_spec, _resident((LRU_BLOCKS, 1, 2 * BW)),
                  _resident((1, R)), _row_chunks(0, D_MODEL, LRU_BLOCKS), _resident((1, D_MODEL))],
        out_specs=row_spec,
        out_shape=jax.ShapeDtypeStruct((rows, D_MODEL), F32),
        scratch_shapes=[pltpu.VMEM((LRU_IN_CHUNKS, D_MODEL, W_CHUNK), BF16),
                        pltpu.VMEM((LRU_BLOCKS, BW, 2 * BW), BF16),
                        pltpu.VMEM((LRU_BLOCKS, W_CHUNK, D_MODEL), BF16),
                        pltpu.VMEM((CONV_HIST, R), F32), pltpu.VMEM((BATCH, R), F32)],
        compiler_params=pltpu.CompilerParams(
            dimension_semantics=("arbitrary",), vmem_limit_bytes=_vmem_limit(est)),
        name="rglru",
    )(h_tm, norm_g.reshape(1, D_MODEL), w_in, b_in.reshape(1, 2 * R), conv_w,
      conv_b.reshape(1, R), gate_w, gate_b.reshape(LRU_BLOCKS, 1, 2 * BW),
      lam.reshape(1, R), w_out, b_out.reshape(1, D_MODEL))


ATT_Q = 256
ATT_SUM_ROWS = V7X_BF16_SUBLANES
ATT_Q_SCALE = math.log2(math.e) * QK_DIM ** -0.5
ATT_AHEAD = 2
assert SEQ % ATT_Q == 0


def _attn_body(q_ref, kv_ref, lam_ref, sg_ref, o_ref, vt_scr, *, lambda_init):
    tq = ATT_Q
    nt = (((1,), (1,)), ((), ()))
    neg = jnp.finfo(F32).min
    lp = lam_ref[...]
    lam = (jnp.exp(jnp.sum(lp[0:1] * lp[1:2], axis=1, keepdims=True))
           - jnp.exp(jnp.sum(lp[2:3] * lp[3:4], axis=1, keepdims=True)) + lambda_init)
    lane = lax.broadcasted_iota(jnp.int32, (tq, V7X_LANES), 1)
    kpos = lax.broadcasted_iota(jnp.int32, (tq, 2 * tq), 0)
    qpos = lax.broadcasted_iota(jnp.int32, (tq, 2 * tq), 1)
    diag_mask = kpos <= jnp.where(qpos >= tq, qpos - tq, qpos)

    for j in range(SEQ // tq):
        rows = slice(j * tq, (j + 1) * tq)
        vt_scr[0:V_DIM, rows] = kv_ref[rows, V7X_LANES:HEAD_KV_LANES].T
    ones_row = lax.broadcasted_iota(jnp.int32, (ATT_SUM_ROWS, SEQ), 0) == 0
    vt_scr[V_DIM:V_DIM + ATT_SUM_ROWS, :] = jnp.where(ones_row, 1.0, 0.0).astype(BF16)

    n_blocks = SEQ // tq
    pairs = [(i, j) for i in range(n_blocks) for j in range(i + 1)]

    def stacked_queries(i):
        q = q_ref[i * tq:(i + 1) * tq, :]
        zero = jnp.zeros_like(q)
        return jnp.concatenate([jnp.where(lane < QK_DIM, q, zero),
                                jnp.where(lane >= QK_DIM, q, zero)], axis=0)

    def scores(qq, i, j):
        s = lax.dot_general(kv_ref[j * tq:(j + 1) * tq, 0:V7X_LANES], qq, nt,
                            preferred_element_type=F32)
        return jnp.where(diag_mask, s, neg) if j == i else s

    qq_of = {}
    s_of = {}

    def issue_scores(n):
        if n < len(pairs):
            i_n, j_n = pairs[n]
            if i_n not in qq_of:
                qq_of.clear()
                qq_of[i_n] = stacked_queries(i_n)
            s_of[n] = scores(qq_of[i_n], i_n, j_n)

    for n in range(ATT_AHEAD):
        issue_scores(n)
    m = acc = None
    for n, (i, j) in enumerate(pairs):
        issue_scores(n + ATT_AHEAD)
        s = s_of.pop(n)
        block_max = jnp.max(s, axis=0, keepdims=True)
        m_new = block_max if j == 0 else jnp.maximum(m, block_max)
        p = jnp.exp2(s - m_new).astype(BF16)
        pv = _dot(vt_scr[:, j * tq:(j + 1) * tq], p)
        acc = pv if j == 0 else jnp.exp2(m - m_new) * acc + pv
        m = m_new
        if j == i:
            on = acc[0:V_DIM] * (1.0 / acc[V_DIM:V_DIM + 1])
            o = (on[:, :tq] - lam * on[:, tq:]).T
            o = _rmsnorm(o, sg_ref[...]) * (1.0 - lambda_init)
            o_ref[i * tq:(i + 1) * tq, :] = o.astype(BF16)


def _attention(q, kv, lam_params, subln_g, lambda_init):
    tq = ATT_Q
    est = (2 * SEQ * HEAD_QK_LANES * 2 * 2 + 2 * SEQ * HEAD_KV_LANES * 2
           + (V_DIM + ATT_SUM_ROWS) * SEQ * 2 + 32 * 2 * tq * tq * 4)
    return pl.pallas_call(
        functools.partial(_attn_body, lambda_init=lambda_init),
        grid=(BATCH, N_HEADS),
        in_specs=[pl.BlockSpec((None, SEQ, HEAD_QK_LANES), lambda b, h: (b, 0, h)),
                  pl.BlockSpec((None, SEQ, HEAD_KV_LANES), lambda b, h: (b, 0, h)),
                  _resident((4, QK_DIM)), _resident((1, V_DIM))],
        out_specs=pl.BlockSpec((None, SEQ, V_DIM), lambda b, h: (b, 0, h)),
        out_shape=jax.ShapeDtypeStruct((BATCH, SEQ, N_HEADS * V_DIM), BF16),
        scratch_shapes=[pltpu.VMEM((V_DIM + ATT_SUM_ROWS, SEQ), BF16)],
        compiler_params=pltpu.CompilerParams(
            dimension_semantics=("arbitrary", "arbitrary"), vmem_limit_bytes=_vmem_limit(est)),
        name="diff_attn",
    )(q, kv, lam_params, subln_g.reshape(1, V_DIM))


def kernel(x, ffn1_norm, ffn1_w_in, ffn1_w_out, mix_norm, ffn2_norm, ffn2_w_in, ffn2_w_out,
           rg_w_in, rg_b_in, rg_conv_w, rg_conv_b, rg_gate_w, rg_gate_b, rg_lambda, rg_w_out, rg_b_out,
           kv_norm, w_kv, diff_w_q, diff_lambda, diff_subln, diff_w_o, final_norm):
    assert x.shape == (BATCH, SEQ, D_MODEL) and DEPTH == 2 and N_A_LAYERS == 1
    rows = BATCH * SEQ
    tables = _rope_tables()

    h = x.reshape(rows, D_MODEL)
    h = _ffn(h, 0, ffn1_norm[0], ffn1_w_in, ffn1_w_out)
    h_tm = jnp.transpose(h.reshape(BATCH, SEQ, D_MODEL), (1, 0, 2)).reshape(rows, D_MODEL)
    h_tm = _rglru(h_tm, mix_norm[0], rg_w_in, rg_b_in[0], rg_conv_w[0], rg_conv_b[0],
                  rg_gate_w, rg_gate_b[0], rg_lambda[0], rg_w_out, rg_b_out[0])
    h = jnp.transpose(h_tm.reshape(SEQ, BATCH, D_MODEL), (1, 0, 2)).reshape(rows, D_MODEL)
    h, kv = _ffn(h, 0, ffn2_norm[0], ffn2_w_in, ffn2_w_out,
                 post=(kv_norm, w_kv[None], tables, HEAD_KV_LANES, 1.0))

    lambda_init = 0.8 - 0.6 * math.exp(-0.3 * 1)
    h, q = _ffn(h, 1, ffn1_norm[1], ffn1_w_in, ffn1_w_out,
                post=(mix_norm[1], diff_w_q, tables, HEAD_QK_LANES, ATT_Q_SCALE))
    att = _attention(q.reshape(BATCH, SEQ, N_HEADS * HEAD_QK_LANES),
                     kv.reshape(BATCH, SEQ, KV_WIDTH), diff_lambda[0], diff_subln[0], lambda_init)
    h = _ffn(h, 1, ffn2_norm[1], ffn2_w_in, ffn2_w_out,
             pre=(att.reshape(rows, N_HEADS * V_DIM), diff_w_o), final_g=final_norm)
    return h.reshape(BATCH, SEQ, D_MODEL)
```

```python
import functools
import math

import jax
import jax.numpy as jnp
from jax import lax
from jax.experimental import pallas as pl
from jax.experimental.pallas import tpu as pltpu

D_MODEL = 1024
BATCH = 8
SEQ = 2048
DEPTH = 2
N_A_LAYERS = DEPTH // 2
D_FF = 2816
LRU_WIDTH = D_MODEL
LRU_BLOCKS = 4
LRU_BLOCK_WIDTH = LRU_WIDTH // LRU_BLOCKS
CONV_WIDTH = 4
RG_C = 8.0
N_HEADS = 8
QK_DIM = 64
V_DIM = 2 * QK_DIM
KV_WIDTH = N_HEADS * (2 * QK_DIM + V_DIM)
ROPE_THETA = 10000.0
EPS = 1e-6

F32 = jnp.float32
BF16 = jnp.bfloat16

V7X_LANES = 128
V7X_SUBLANES = 8
V7X_BF16_SUBLANES = 16
V7X_VMEM_BYTES = 64 * 1024 * 1024
V7X_MXU_DIM = 256

HEAD_QK_LANES = 2 * QK_DIM
HEAD_KV_LANES = 2 * QK_DIM + V_DIM
assert HEAD_QK_LANES == V7X_LANES and V_DIM == V7X_LANES

W_CHUNK = V7X_MXU_DIM


def _vmem_limit(est_bytes):
    return int(min(est_bytes * 5 // 4 + (8 << 20), V7X_VMEM_BYTES - (4 << 20)))


def _resident(shape):
    nd = len(shape)
    return pl.BlockSpec(shape, lambda *_: (0,) * nd, pipeline_mode=pl.Buffered(1))


def _col_chunks(layer, k, n_chunks):
    return pl.BlockSpec((None, k, W_CHUNK), lambda s: (layer, 0, jnp.minimum(s, n_chunks - 1)))


def _row_chunks(layer, n, n_chunks):
    return pl.BlockSpec((None, W_CHUNK, n), lambda s: (layer, jnp.minimum(s, n_chunks - 1), 0))


def _cast_chunk(step, n_chunks, w_ref, w_scr):
    @pl.when(step < n_chunks)
    def _():
        w_scr[step] = w_ref[...].astype(BF16)


def _rmsnorm(x, g):
    ms = jnp.mean(x * x, axis=-1, keepdims=True)
    return x * lax.rsqrt(ms + EPS) * g


def _dot(a, b):
    return jnp.dot(a, b, preferred_element_type=F32)


def _rope_tables():
    pos = jnp.arange(SEQ, dtype=F32)
    inv_freq = ROPE_THETA ** (-jnp.arange(0, QK_DIM, 2, dtype=F32) / QK_DIM)
    ang = pos[:, None] * inv_freq[None, :]
    cos, sin = jnp.cos(ang), jnp.sin(ang)
    zero = jnp.zeros_like(sin)
    cos_t = jnp.concatenate([cos, cos, cos, cos], axis=-1)
    sin_lo = jnp.concatenate([-sin, zero, -sin, zero], axis=-1)
    sin_hi = jnp.concatenate([zero, sin, zero, sin], axis=-1)
    return cos_t, sin_lo, sin_hi


def _rope_store(y, cos, slo, shi, o_ref, col0, head_lanes, scale):
    half = QK_DIM // 2
    for base in range(0, y.shape[1], head_lanes):
        t = y[:, base:base + V7X_LANES]
        r = (t * cos + pltpu.roll(t, V7X_LANES - half, 1) * slo + pltpu.roll(t, half, 1) * shi)
        o_ref[:, col0 + base:col0 + base + V7X_LANES] = (r * scale).astype(BF16)
        if head_lanes > V7X_LANES:
            o_ref[:, col0 + base + V7X_LANES:col0 + base + head_lanes] = (
                y[:, base + V7X_LANES:base + head_lanes].astype(BF16))


FFN_ROWS = 512
FFN_IN_CHUNKS = 2 * D_FF // W_CHUNK
FFN_OUT_CHUNKS = D_FF // W_CHUNK
FFN_PROLOGUE = FFN_IN_CHUNKS
assert D_FF % W_CHUNK == 0 and SEQ % FFN_ROWS == 0 and D_MODEL % W_CHUNK == 0


def _ffn_body(*refs, pre_chunks, post_chunks, post_head_lanes, post_scale, final_norm):
    refs = list(refs)
    take = lambda k: [refs.pop(0) for _ in range(k)]
    (h_ref,) = take(1)
    if pre_chunks:
        a_ref, wpre_ref = take(2)
    g_ref, win_ref, wout_ref = take(3)
    if post_chunks:
        pg_ref, wpost_ref, cos_ref, slo_ref, shi_ref = take(5)
    if final_norm:
        (fg_ref,) = take(1)
    (o_ref,) = take(1)
    if post_chunks:
        (op_ref,) = take(1)
    win_scr, wout_scr = take(2)
    if pre_chunks:
        (wpre_scr,) = take(1)
    if post_chunks:
        (wpost_scr,) = take(1)

    step = pl.program_id(0)
    _cast_chunk(step, FFN_IN_CHUNKS, win_ref, win_scr)
    _cast_chunk(step, FFN_OUT_CHUNKS, wout_ref, wout_scr)
    if pre_chunks:
        _cast_chunk(step, pre_chunks, wpre_ref, wpre_scr)
    if post_chunks:
        _cast_chunk(step, post_chunks, wpost_ref, wpost_scr)

    @pl.when(step >= FFN_PROLOGUE)
    def _():
        x = h_ref[...]
        if pre_chunks:
            for c in range(pre_chunks):
                x = x + _dot(a_ref[:, c * W_CHUNK:(c + 1) * W_CHUNK], wpre_scr[c])
        u = _rmsnorm(x, g_ref[...]).astype(BF16)
        acc = None
        for c in range(FFN_OUT_CHUNKS):
            gate = _dot(u, win_scr[c])
            up = _dot(u, win_scr[FFN_OUT_CHUNKS + c])
            act = (gate * jax.nn.sigmoid(gate) * up).astype(BF16)
            part = _dot(act, wout_scr[c])
            acc = part if acc is None else acc + part
        y = x + 0.5 * acc
        if post_chunks:
            un = _rmsnorm(y, pg_ref[...]).astype(BF16)
            cos, slo, shi = cos_ref[...], slo_ref[...], shi_ref[...]
            for c in range(post_chunks):
                _rope_store(_dot(un, wpost_scr[c]), cos, slo, shi, op_ref, c * W_CHUNK,
                            post_head_lanes, post_scale)
        o_ref[...] = _rmsnorm(y, fg_ref[...]) if final_norm else y


def _ffn(h, layer, norm_g, w_in, w_out, *, pre=None, post=None, final_g=None):
    rows = h.shape[0]
    tm = FFN_ROWS
    tile = lambda s: jnp.maximum(s - FFN_PROLOGUE, 0)
    row_spec = pl.BlockSpec((tm, D_MODEL), lambda s: (tile(s), 0))
    vec_spec = _resident((1, D_MODEL))
    in_specs, args = [row_spec], [h]
    scratch = [pltpu.VMEM((FFN_IN_CHUNKS, D_MODEL, W_CHUNK), BF16),
               pltpu.VMEM((FFN_OUT_CHUNKS, W_CHUNK, D_MODEL), BF16)]
    est = (3 * D_MODEL * D_FF * 2 + 4 * D_MODEL * W_CHUNK * 4 + 6 * tm * D_MODEL * 4
           + 6 * tm * W_CHUNK * 4)
    pre_chunks = post_chunks = post_head_lanes = 0
    post_scale = 1.0
    if pre is not None:
        a, w_pre = pre
        n_pre = a.shape[1]
        pre_chunks = n_pre // W_CHUNK
        in_specs += [pl.BlockSpec((tm, n_pre), lambda s: (tile(s), 0)),
                     _row_chunks(0, D_MODEL, pre_chunks)]
        args += [a, w_pre]
        est += n_pre * D_MODEL * 2 + 2 * tm * n_pre * 2 + 2 * W_CHUNK * D_MODEL * 4
    in_specs += [vec_spec, _col_chunks(layer, D_MODEL, FFN_IN_CHUNKS),
                 _row_chunks(layer, D_MODEL, FFN_OUT_CHUNKS)]
    args += [norm_g.reshape(1, D_MODEL), w_in, w_out]
    out_specs, out_shape = [row_spec], [jax.ShapeDtypeStruct((rows, D_MODEL), F32)]
    if post is not None:
        post_g, w_post, tables, post_head_lanes, post_scale = post
        n_post = w_post.shape[2]
        post_chunks = n_post // W_CHUNK
        seq_tiles = SEQ // tm
        tab_spec = pl.BlockSpec((tm, V7X_LANES), lambda s: (tile(s) % seq_tiles, 0))
        in_specs += [vec_spec, _col_chunks(0, D_MODEL, post_chunks), tab_spec, tab_spec, tab_spec]
        args += [post_g.reshape(1, D_MODEL), w_post, *tables]
        out_specs.append(pl.BlockSpec((tm, n_post), lambda s: (tile(s), 0)))
        out_shape.append(jax.ShapeDtypeStruct((rows, n_post), BF16))
        est += (D_MODEL * n_post * 2 + 2 * tm * n_post * 2 + 2 * D_MODEL * W_CHUNK * 4
                + 6 * tm * V7X_LANES * 4)
    if final_g is not None:
        in_specs.append(vec_spec)
        args.append(final_g.reshape(1, D_MODEL))
    if pre is not None:
        scratch.append(pltpu.VMEM((pre_chunks, W_CHUNK, D_MODEL), BF16))
    if post is not None:
        scratch.append(pltpu.VMEM((post_chunks, D_MODEL, W_CHUNK), BF16))
    assert max(pre_chunks, post_chunks, FFN_OUT_CHUNKS) <= FFN_PROLOGUE
    out = pl.pallas_call(
        functools.partial(_ffn_body, pre_chunks=pre_chunks, post_chunks=post_chunks,
                          post_head_lanes=post_head_lanes, post_scale=post_scale,
                          final_norm=final_g is not None),
        grid=(FFN_PROLOGUE + rows // tm,),
        in_specs=in_specs,
        out_specs=out_specs,
        out_shape=out_shape,
        scratch_shapes=scratch,
        compiler_params=pltpu.CompilerParams(
            dimension_semantics=("arbitrary",), vmem_limit_bytes=_vmem_limit(est)),
        name="ffn",
    )(*args)
    return out if post is not None else out[0]


LRU_STEPS = 64
LRU_ROWS = LRU_STEPS * BATCH
CONV_HIST = (CONV_WIDTH - 1) * BATCH
LRU_IN_CHUNKS = 2 * LRU_WIDTH // W_CHUNK
LRU_PROLOGUE = LRU_IN_CHUNKS
assert BATCH == V7X_SUBLANES and LRU_ROWS >= CONV_HIST and LRU_BLOCK_WIDTH == W_CHUNK


def _swap_major(v, n_outer, n_inner):
    width = v.shape[-1]
    return jnp.swapaxes(v.reshape(n_outer, n_inner, width), 0, 1).reshape(n_outer * n_inner, width)


def _rglru_body(h_ref, g_ref, win_ref, bin_ref, cw_ref, cb_ref, gw_ref, gb_ref, lam_ref,
                wout_ref, bout_ref, o_ref, win_scr, gw_scr, wout_scr, hist_scr, state_scr):
    tm = LRU_ROWS
    R = LRU_WIDTH
    BW = LRU_BLOCK_WIDTH

    step = pl.program_id(0)
    _cast_chunk(step, LRU_IN_CHUNKS, win_ref, win_scr)
    _cast_chunk(step, LRU_BLOCKS, gw_ref, gw_scr)
    _cast_chunk(step, LRU_BLOCKS, wout_ref, wout_scr)

    @pl.when(step == LRU_PROLOGUE)
    def _():
        hist_scr[...] = jnp.zeros((CONV_HIST, R), F32)
        state_scr[...] = jnp.zeros((BATCH, R), F32)

    def block_cols(n, base=0):
        return slice(base + n * BW, base + (n + 1) * BW)

    @pl.when(step >= LRU_PROLOGUE)
    def _():
        x = h_ref[...].reshape(tm, D_MODEL)
        u = _rmsnorm(x, g_ref[...]).astype(BF16)
        lam = lam_ref[...]
        log_sig = jnp.minimum(lam, 0.0) - jnp.log1p(jnp.exp(-jnp.abs(lam)))
        c_exp = (RG_C * math.log2(math.e)) * log_sig
        c_tanh = -RG_C * log_sig

        def in_proj(n):
            y_gate = _dot(u, win_scr[n]) + bin_ref[:, block_cols(n)]
            y_x = _dot(u, win_scr[LRU_BLOCKS + n]) + bin_ref[:, block_cols(n, R)]
            return y_gate, y_x

        def conv_gates(n, y_x):
            cols = block_cols(n)
            y_x = _swap_major(y_x, BATCH, LRU_STEPS)
            x_hist = jnp.concatenate([hist_scr[:, cols], y_x], axis=0)
            hist_scr[:, cols] = y_x[tm - CONV_HIST:, :]
            xn = cb_ref[:, cols]
            for k in range(CONV_WIDTH):
                xn = xn + x_hist[k * BATCH:k * BATCH + tm, :] * cw_ref[k:k + 1, cols]
            gates = jax.nn.sigmoid(_dot(xn.astype(BF16), gw_scr[n]) + gb_ref[n])
            return xn, gates

        def recur_out(n, y_gate, xn, gates):
            cols = block_cols(n)
            gate_a = gates[:, BW:]
            a = jnp.exp2(gate_a * c_exp[:, cols])
            z = jnp.tanh(gate_a * c_tanh[:, cols]) * (1.0 + a * a)
            b = jnp.where(z > 0.0, z * lax.rsqrt(z), 0.0) * (gates[:, :BW] * xn)
            h = state_scr[:, cols]
            hs = []
            for t in range(LRU_STEPS):
                r = slice(t * BATCH, (t + 1) * BATCH)
                h = a[r, :] * h + b[r, :]
                hs.append(h)
            state_scr[:, cols] = h
            c1 = math.sqrt(2.0 / math.pi)
            t = jnp.tanh(y_gate * (c1 + (c1 * 0.044715) * (y_gate * y_gate)))
            w = (0.5 * y_gate) * _swap_major(jnp.concatenate(hs, axis=0), LRU_STEPS, BATCH)
            return _dot((w + w * t).astype(BF16), wout_scr[n])

        ys, cg = {}, {}
        out = None
        for s in range(LRU_BLOCKS + 2):
            if s < LRU_BLOCKS:
                ys[s] = in_proj(s)
            if 0 <= s - 1 < LRU_BLOCKS:
                cg[s - 1] = conv_gates(s - 1, ys[s - 1][1])
            if 0 <= s - 2 < LRU_BLOCKS:
                part = recur_out(s - 2, ys.pop(s - 2)[0], *cg.pop(s - 2))
                out = part if out is None else out + part
        o_ref[...] = (x + (out + bout_ref[...])).reshape(BATCH, LRU_STEPS, D_MODEL)


def _rglru(h, norm_g, w_in, b_in, conv_w, conv_b, gate_w, gate_b, lam, w_out, b_out):
    tm = LRU_ROWS
    R = LRU_WIDTH
    BW = LRU_BLOCK_WIDTH
    row_spec = pl.BlockSpec((BATCH, LRU_STEPS, D_MODEL),
                            lambda s: (0, jnp.maximum(s - LRU_PROLOGUE, 0), 0))
    gate_spec = pl.BlockSpec((None, None, BW, 2 * BW),
                             lambda s: (0, jnp.minimum(s, LRU_BLOCKS - 1), 0, 0))
    est = ((D_MODEL * 2 * R + R * D_MODEL + LRU_BLOCKS * BW * 2 * BW) * 2
           + 2 * (D_MODEL * W_CHUNK + BW * 2 * BW + W_CHUNK * D_MODEL) * 4
           + 6 * tm * D_MODEL * 4 + 16 * tm * BW * 4)
    return pl.pallas_call(
        _rglru_body,
        grid=(LRU_PROLOGUE + SEQ // LRU_STEPS,),
        in_specs=[row_spec, _resident((1, D_MODEL)), _col_chunks(0, D_MODEL, LRU_IN_CHUNKS),
                  _resident((1, 2 * R)), _resident((CONV_WIDTH, R)), _resident((1, R)),
                  gate_spec, _resident((LRU_BLOCKS, 1, 2 * BW)),
                  _resident((1, R)), _row_chunks(0, D_MODEL, LRU_BLOCKS), _resident((1, D_MODEL))],
        out_specs=row_spec,
        out_shape=jax.ShapeDtypeStruct((BATCH, SEQ, D_MODEL), F32),
        scratch_shapes=[pltpu.VMEM((LRU_IN_CHUNKS, D_MODEL, W_CHUNK), BF16),
                        pltpu.VMEM((LRU_BLOCKS, BW, 2 * BW), BF16),
                        pltpu.VMEM((LRU_BLOCKS, W_CHUNK, D_MODEL), BF16),
                        pltpu.VMEM((CONV_HIST, R), F32), pltpu.VMEM((BATCH, R), F32)],
        compiler_params=pltpu.CompilerParams(
            dimension_semantics=("arbitrary",), vmem_limit_bytes=_vmem_limit(est)),
        name="rglru",
    )(h, norm_g.reshape(1, D_MODEL), w_in, b_in.reshape(1, 2 * R), conv_w,
      conv_b.reshape(1, R), gate_w, gate_b.reshape(LRU_BLOCKS, 1, 2 * BW),
      lam.reshape(1, R), w_out, b_out.reshape(1, D_MODEL))


ATT_Q = 256
ATT_SUM_ROWS = V7X_BF16_SUBLANES
ATT_Q_SCALE = math.log2(math.e) * QK_DIM ** -0.5
ATT_AHEAD = 2
assert SEQ % ATT_Q == 0


def _attn_body(q_ref, kv_ref, lam_ref, sg_ref, o_ref, vt_scr, *, lambda_init):
    tq = ATT_Q
    nt = (((1,), (1,)), ((), ()))
    neg = jnp.finfo(F32).min
    lp = lam_ref[...]
    lam = (jnp.exp(jnp.sum(lp[0:1] * lp[1:2], axis=1, keepdims=True))
           - jnp.exp(jnp.sum(lp[2:3] * lp[3:4], axis=1, keepdims=True)) + lambda_init)
    lane = lax.broadcasted_iota(jnp.int32, (tq, V7X_LANES), 1)
    kpos = lax.broadcasted_iota(jnp.int32, (tq, 2 * tq), 0)
    qpos = lax.broadcasted_iota(jnp.int32, (tq, 2 * tq), 1)
    diag_mask = kpos <= jnp.where(qpos >= tq, qpos - tq, qpos)

    for j in range(SEQ // tq):
        rows = slice(j * tq, (j + 1) * tq)
        vt_scr[0:V_DIM, rows] = kv_ref[rows, V7X_LANES:HEAD_KV_LANES].T
    ones_row = lax.broadcasted_iota(jnp.int32, (ATT_SUM_ROWS, SEQ), 0) == 0
    vt_scr[V_DIM:V_DIM + ATT_SUM_ROWS, :] = jnp.where(ones_row, 1.0, 0.0).astype(BF16)

    n_blocks = SEQ // tq
    pairs = [(i, j) for i in range(n_blocks) for j in range(i + 1)]

    def stacked_queries(i):
        q = q_ref[i * tq:(i + 1) * tq, :]
        zero = jnp.zeros_like(q)
        return jnp.concatenate([jnp.where(lane < QK_DIM, q, zero),
                                jnp.where(lane >= QK_DIM, q, zero)], axis=0)

    def scores(qq, i, j):
        s = lax.dot_general(kv_ref[j * tq:(j + 1) * tq, 0:V7X_LANES], qq, nt,
                            preferred_element_type=F32)
        return jnp.where(diag_mask, s, neg) if j == i else s

    qq_of = {}
    s_of = {}

    def issue_scores(n):
        if n < len(pairs):
            i_n, j_n = pairs[n]
            if i_n not in qq_of:
                qq_of.clear()
                qq_of[i_n] = stacked_queries(i_n)
            s_of[n] = scores(qq_of[i_n], i_n, j_n)

    for n in range(ATT_AHEAD):
        issue_scores(n)
    m = acc = None
    for n, (i, j) in enumerate(pairs):
        issue_scores(n + ATT_AHEAD)
        s = s_of.pop(n)
        block_max = jnp.max(s, axis=0, keepdims=True)
        m_new = block_max if j == 0 else jnp.maximum(m, block_max)
        p = jnp.exp2(s - m_new).astype(BF16)
        pv = _dot(vt_scr[:, j * tq:(j + 1) * tq], p)
        acc = pv if j == 0 else jnp.exp2(m - m_new) * acc + pv
        m = m_new
        if j == i:
            on = acc[0:V_DIM] * (1.0 / acc[V_DIM:V_DIM + 1])
            o = (on[:, :tq] - lam * on[:, tq:]).T
            o = _rmsnorm(o, sg_ref[...]) * (1.0 - lambda_init)
            o_ref[i * tq:(i + 1) * tq, :] = o.astype(BF16)


def _attention(q, kv, lam_params, subln_g, lambda_init):
    tq = ATT_Q
    est = (2 * SEQ * HEAD_QK_LANES * 2 * 2 + 2 * SEQ * HEAD_KV_LANES * 2
           + (V_DIM + ATT_SUM_ROWS) * SEQ * 2 + 32 * 2 * tq * tq * 4)
    return pl.pallas_call(
        functools.partial(_attn_body, lambda_init=lambda_init),
        grid=(BATCH, N_HEADS),
        in_specs=[pl.BlockSpec((None, SEQ, HEAD_QK_LANES), lambda b, h: (b, 0, h)),
                  pl.BlockSpec((None, SEQ, HEAD_KV_LANES), lambda b, h: (b, 0, h)),
                  _resident((4, QK_DIM)), _resident((1, V_DIM))],
        out_specs=pl.BlockSpec((None, SEQ, V_DIM), lambda b, h: (b, 0, h)),
        out_shape=jax.ShapeDtypeStruct((BATCH, SEQ, N_HEADS * V_DIM), BF16),
        scratch_shapes=[pltpu.VMEM((V_DIM + ATT_SUM_ROWS, SEQ), BF16)],
        compiler_params=pltpu.CompilerParams(
            dimension_semantics=("arbitrary", "arbitrary"), vmem_limit_bytes=_vmem_limit(est)),
        name="diff_attn",
    )(q, kv, lam_params, subln_g.reshape(1, V_DIM))


def kernel(x, ffn1_norm, ffn1_w_in, ffn1_w_out, mix_norm, ffn2_norm, ffn2_w_in, ffn2_w_out,
           rg_w_in, rg_b_in, rg_conv_w, rg_conv_b, rg_gate_w, rg_gate_b, rg_lambda, rg_w_out, rg_b_out,
           kv_norm, w_kv, diff_w_q, diff_lambda, diff_subln, diff_w_o, final_norm):
    assert x.shape == (BATCH, SEQ, D_MODEL) and DEPTH == 2 and N_A_LAYERS == 1
    rows = BATCH * SEQ
    tables = _rope_tables()

    h = x.reshape(rows, D_MODEL)
    h = _ffn(h, 0, ffn1_norm[0], ffn1_w_in, ffn1_w_out)
    h = _rglru(h.reshape(BATCH, SEQ, D_MODEL), mix_norm[0], rg_w_in, rg_b_in[0], rg_conv_w[0],
               rg_conv_b[0], rg_gate_w, rg_gate_b[0], rg_lambda[0], rg_w_out, rg_b_out[0])
    h = h.reshape(rows, D_MODEL)
    h, kv = _ffn(h, 0, ffn2_norm[0], ffn2_w_in, ffn2_w_out,
                 post=(kv_norm, w_kv[None], tables, HEAD_KV_LANES, 1.0))

    lambda_init = 0.8 - 0.6 * math.exp(-0.3 * 1)
    h, q = _ffn(h, 1, ffn1_norm[1], ffn1_w_in, ffn1_w_out,
                post=(mix_norm[1], diff_w_q, tables, HEAD_QK_LANES, ATT_Q_SCALE))
    att = _attention(q.reshape(BATCH, SEQ, N_HEADS * HEAD_QK_LANES),
                     kv.reshape(BATCH, SEQ, KV_WIDTH), diff_lambda[0], diff_subln[0], lambda_init)
    h = _ffn(h, 1, ffn2_norm[1], ffn2_w_in, ffn2_w_out,
             pre=(att.reshape(rows, N_HEADS * V_DIM), diff_w_o), final_g=final_norm)
    return h.reshape(BATCH, SEQ, D_MODEL)
```

```python
import functools
import math

import jax
import jax.numpy as jnp
from jax import lax
from jax.experimental import pallas as pl
from jax.experimental.pallas import tpu as pltpu

D_MODEL = 1024
BATCH = 8
SEQ = 2048
DEPTH = 2
N_A_LAYERS = DEPTH // 2
D_FF = 2816
LRU_WIDTH = D_MODEL
LRU_BLOCKS = 4
LRU_BLOCK_WIDTH = LRU_WIDTH // LRU_BLOCKS
CONV_WIDTH = 4
RG_C = 8.0
N_HEADS = 8
QK_DIM = 64
V_DIM = 2 * QK_DIM
KV_WIDTH = N_HEADS * (2 * QK_DIM + V_DIM)
ROPE_THETA = 10000.0
EPS = 1e-6

F32 = jnp.float32
BF16 = jnp.bfloat16

V7X_LANES = 128
V7X_SUBLANES = 8
V7X_BF16_SUBLANES = 16
V7X_VMEM_BYTES = 64 * 1024 * 1024
V7X_MXU_DIM = 256

HEAD_QK_LANES = 2 * QK_DIM
HEAD_KV_LANES = 2 * QK_DIM + V_DIM
assert HEAD_QK_LANES == V7X_LANES and V_DIM == V7X_LANES

W_CHUNK = V7X_MXU_DIM


def _vmem_limit(est_bytes):
    return int(min(est_bytes * 5 // 4 + (8 << 20), V7X_VMEM_BYTES - (4 << 20)))


def _resident(shape):
    nd = len(shape)
    return pl.BlockSpec(shape, lambda *_: (0,) * nd, pipeline_mode=pl.Buffered(1))


def _col_chunks(layer, k, n_chunks):
    return pl.BlockSpec((None, k, W_CHUNK), lambda s: (layer, 0, jnp.minimum(s, n_chunks - 1)))


def _row_chunks(layer, n, n_chunks):
    return pl.BlockSpec((None, W_CHUNK, n), lambda s: (layer, jnp.minimum(s, n_chunks - 1), 0))


def _cast_chunk(step, n_chunks, w_ref, w_scr):
    @pl.when(step < n_chunks)
    def _():
        w_scr[step] = w_ref[...].astype(BF16)


def _rmsnorm(x, g):
    ms = jnp.mean(x * x, axis=-1, keepdims=True)
    return x * lax.rsqrt(ms + EPS) * g


def _dot(a, b):
    return jnp.dot(a, b, preferred_element_type=F32)


def _rope_tables():
    pos = jnp.arange(SEQ, dtype=F32)
    inv_freq = ROPE_THETA ** (-jnp.arange(0, QK_DIM, 2, dtype=F32) / QK_DIM)
    ang = pos[:, None] * inv_freq[None, :]
    cos, sin = jnp.cos(ang), jnp.sin(ang)
    zero = jnp.zeros_like(sin)
    cos_t = jnp.concatenate([cos, cos, cos, cos], axis=-1)
    sin_lo = jnp.concatenate([-sin, zero, -sin, zero], axis=-1)
    sin_hi = jnp.concatenate([zero, sin, zero, sin], axis=-1)
    return cos_t, sin_lo, sin_hi


def _rope_store(y, cos, slo, shi, o_ref, col0, head_lanes, scale):
    half = QK_DIM // 2
    for base in range(0, y.shape[1], head_lanes):
        t = y[:, base:base + V7X_LANES]
        r = (t * cos + pltpu.roll(t, V7X_LANES - half, 1) * slo + pltpu.roll(t, half, 1) * shi)
        o_ref[:, col0 + base:col0 + base + V7X_LANES] = (r * scale).astype(BF16)
        if head_lanes > V7X_LANES:
            o_ref[:, col0 + base + V7X_LANES:col0 + base + head_lanes] = (
                y[:, base + V7X_LANES:base + head_lanes].astype(BF16))


FFN_ROWS = 512
FFN_IN_CHUNKS = 2 * D_FF // W_CHUNK
FFN_OUT_CHUNKS = D_FF // W_CHUNK
FFN_PROLOGUE = FFN_IN_CHUNKS
assert D_FF % W_CHUNK == 0 and SEQ % FFN_ROWS == 0 and D_MODEL % W_CHUNK == 0


def _ffn_body(*refs, pre_chunks, post_chunks, post_head_lanes, post_scale, final_norm):
    refs = list(refs)
    take = lambda k: [refs.pop(0) for _ in range(k)]
    (h_ref,) = take(1)
    if pre_chunks:
        a_ref, wpre_ref = take(2)
    g_ref, win_ref, wout_ref = take(3)
    if post_chunks:
        pg_ref, wpost_ref, cos_ref, slo_ref, shi_ref = take(5)
    if final_norm:
        (fg_ref,) = take(1)
    (o_ref,) = take(1)
    if post_chunks:
        (op_ref,) = take(1)
    win_scr, wout_scr = take(2)
    if pre_chunks:
        (wpre_scr,) = take(1)
    if post_chunks:
        (wpost_scr,) = take(1)

    step = pl.program_id(0)
    _cast_chunk(step, FFN_IN_CHUNKS, win_ref, win_scr)
    _cast_chunk(step, FFN_OUT_CHUNKS, wout_ref, wout_scr)
    if pre_chunks:
        _cast_chunk(step, pre_chunks, wpre_ref, wpre_scr)
    if post_chunks:
        _cast_chunk(step, post_chunks, wpost_ref, wpost_scr)

    @pl.when(step >= FFN_PROLOGUE)
    def _():
        x = h_ref[...]
        if pre_chunks:
            for c in range(pre_chunks):
                x = x + _dot(a_ref[:, c * W_CHUNK:(c + 1) * W_CHUNK], wpre_scr[c])
        u = _rmsnorm(x, g_ref[...]).astype(BF16)
        acc = None
        for c in range(FFN_OUT_CHUNKS):
            gate = _dot(u, win_scr[c])
            up = _dot(u, win_scr[FFN_OUT_CHUNKS + c])
            act = (gate * jax.nn.sigmoid(gate) * up).astype(BF16)
            part = _dot(act, wout_scr[c])
            acc = part if acc is None else acc + part
        y = x + 0.5 * acc
        if post_chunks:
            un = _rmsnorm(y, pg_ref[...]).astype(BF16)
            cos, slo, shi = cos_ref[...], slo_ref[...], shi_ref[...]
            for c in range(post_chunks):
                _rope_store(_dot(un, wpost_scr[c]), cos, slo, shi, op_ref, c * W_CHUNK,
                            post_head_lanes, post_scale)
        o_ref[...] = _rmsnorm(y, fg_ref[...]) if final_norm else y


def _ffn(h, layer, norm_g, w_in, w_out, *, pre=None, post=None, final_g=None):
    rows = h.shape[0]
    tm = FFN_ROWS
    tile = lambda s: jnp.maximum(s - FFN_PROLOGUE, 0)
    row_spec = pl.BlockSpec((tm, D_MODEL), lambda s: (tile(s), 0))
    vec_spec = _resident((1, D_MODEL))
    in_specs, args = [row_spec], [h]
    scratch = [pltpu.VMEM((FFN_IN_CHUNKS, D_MODEL, W_CHUNK), BF16),
               pltpu.VMEM((FFN_OUT_CHUNKS, W_CHUNK, D_MODEL), BF16)]
    est = (3 * D_MODEL * D_FF * 2 + 4 * D_MODEL * W_CHUNK * 4 + 6 * tm * D_MODEL * 4
           + 6 * tm * W_CHUNK * 4)
    pre_chunks = post_chunks = post_head_lanes = 0
    post_scale = 1.0
    if pre is not None:
        a, w_pre = pre
        n_pre = a.shape[1]
        pre_chunks = n_pre // W_CHUNK
        in_specs += [pl.BlockSpec((tm, n_pre), lambda s: (tile(s), 0)),
                     _row_chunks(0, D_MODEL, pre_chunks)]
        args += [a, w_pre]
        est += n_pre * D_MODEL * 2 + 2 * tm * n_pre * 2 + 2 * W_CHUNK * D_MODEL * 4
    in_specs += [vec_spec, _col_chunks(layer, D_MODEL, FFN_IN_CHUNKS),
                 _row_chunks(layer, D_MODEL, FFN_OUT_CHUNKS)]
    args += [norm_g.reshape(1, D_MODEL), w_in, w_out]
    out_specs, out_shape = [row_spec], [jax.ShapeDtypeStruct((rows, D_MODEL), F32)]
    if post is not None:
        post_g, w_post, tables, post_head_lanes, post_scale = post
        n_post = w_post.shape[2]
        post_chunks = n_post // W_CHUNK
        seq_tiles = SEQ // tm
        tab_spec = pl.BlockSpec((tm, V7X_LANES), lambda s: (tile(s) % seq_tiles, 0))
        in_specs += [vec_spec, _col_chunks(0, D_MODEL, post_chunks), tab_spec, tab_spec, tab_spec]
        args += [post_g.reshape(1, D_MODEL), w_post, *tables]
        out_specs.append(pl.BlockSpec((tm, n_post), lambda s: (tile(s), 0)))
        out_shape.append(jax.ShapeDtypeStruct((rows, n_post), BF16))
        est += (D_MODEL * n_post * 2 + 2 * tm * n_post * 2 + 2 * D_MODEL * W_CHUNK * 4
                + 6 * tm * V7X_LANES * 4)
    if final_g is not None:
        in_specs.append(vec_spec)
        args.append(final_g.reshape(1, D_MODEL))
    if pre is not None:
        scratch.append(pltpu.VMEM((pre_chunks, W_CHUNK, D_MODEL), BF16))
    if post is not None:
        scratch.append(pltpu.VMEM((post_chunks, D_MODEL, W_CHUNK), BF16))
    assert max(pre_chunks, post_chunks, FFN_OUT_CHUNKS) <= FFN_PROLOGUE
    out = pl.pallas_call(
        functools.partial(_ffn_body, pre_chunks=pre_chunks, post_chunks=post_chunks,
                          post_head_lanes=post_head_lanes, post_scale=post_scale,
                          final_norm=final_g is not None),
        grid=(FFN_PROLOGUE + rows // tm,),
        in_specs=in_specs,
        out_specs=out_specs,
        out_shape=out_shape,
        scratch_shapes=scratch,
        compiler_params=pltpu.CompilerParams(
            dimension_semantics=("arbitrary",), vmem_limit_bytes=_vmem_limit(est)),
        name="ffn",
    )(*args)
    return out if post is not None else out[0]


LRU_STEPS = 64
LRU_ROWS = LRU_STEPS * BATCH
CONV_HIST = (CONV_WIDTH - 1) * BATCH
LRU_IN_CHUNKS = 2 * LRU_WIDTH // W_CHUNK
LRU_PROLOGUE = LRU_IN_CHUNKS
assert BATCH == V7X_SUBLANES and LRU_ROWS >= CONV_HIST and LRU_BLOCK_WIDTH == W_CHUNK


def _swap_major(v, n_outer, n_inner):
    width = v.shape[-1]
    return jnp.swapaxes(v.reshape(n_outer, n_inner, width), 0, 1).reshape(n_outer * n_inner, width)


def _rglru_body(h_ref, g_ref, win_ref, bin_ref, cw_ref, cb_ref, gw_ref, gb_ref, lam_ref,
                wout_ref, bout_ref, o_ref, win_scr, gw_scr, wout_scr, hist_scr, state_scr):
    tm = LRU_ROWS
    R = LRU_WIDTH
    BW = LRU_BLOCK_WIDTH

    step = pl.program_id(0)
    _cast_chunk(step, LRU_IN_CHUNKS, win_ref, win_scr)
    _cast_chunk(step, LRU_BLOCKS, gw_ref, gw_scr)
    _cast_chunk(step, LRU_BLOCKS, wout_ref, wout_scr)

    @pl.when(step == LRU_PROLOGUE)
    def _():
        hist_scr[...] = jnp.zeros((CONV_HIST, R), F32)
        state_scr[...] = jnp.zeros((BATCH, R), F32)

    def block_cols(n, base=0):
        return slice(base + n * BW, base + (n + 1) * BW)

    @pl.when(step >= LRU_PROLOGUE)
    def _():
        x = h_ref[...].reshape(tm, D_MODEL)
        u = _rmsnorm(x, g_ref[...]).astype(BF16)
        lam = lam_ref[...]
        log_sig = jnp.minimum(lam, 0.0) - jnp.log1p(jnp.exp(-jnp.abs(lam)))
        c_exp = (RG_C * math.log2(math.e)) * log_sig
        c_tanh = -RG_C * log_sig

        def in_proj(n):
            y_gate = _dot(u, win_scr[n]) + bin_ref[:, block_cols(n)]
            y_x = _dot(u, win_scr[LRU_BLOCKS + n]) + bin_ref[:, block_cols(n, R)]
            return y_gate, y_x

        def conv_gates(n, y_x):
            cols = block_cols(n)
            y_x = _swap_major(y_x, BATCH, LRU_STEPS)
            x_hist = jnp.concatenate([hist_scr[:, cols], y_x], axis=0)
            hist_scr[:, cols] = y_x[tm - CONV_HIST:, :]
            xn = cb_ref[:, cols]
            for k in range(CONV_WIDTH):
                xn = xn + x_hist[k * BATCH:k * BATCH + tm, :] * cw_ref[k:k + 1, cols]
            gates = jax.nn.sigmoid(_dot(xn.astype(BF16), gw_scr[n]) + gb_ref[n])
            return xn, gates

        def recur_out(n, y_gate, xn, gates):
            cols = block_cols(n)
            gate_a = gates[:, BW:]
            a = jnp.exp2(gate_a * c_exp[:, cols])
            z = jnp.tanh(gate_a * c_tanh[:, cols]) * (1.0 + a * a)
            b = jnp.where(z > 0.0, z * lax.rsqrt(z), 0.0) * (gates[:, :BW] * xn)
            h = state_scr[:, cols]
            hs = []
            for t in range(LRU_STEPS):
                r = slice(t * BATCH, (t + 1) * BATCH)
                h = a[r, :] * h + b[r, :]
                hs.append(h)
            state_scr[:, cols] = h
            c1 = math.sqrt(2.0 / math.pi)
            t = jnp.tanh(y_gate * (c1 + (c1 * 0.044715) * (y_gate * y_gate)))
            w = (0.5 * y_gate) * _swap_major(jnp.concatenate(hs, axis=0), LRU_STEPS, BATCH)
            return _dot((w + w * t).astype(BF16), wout_scr[n])

        ys, cg = {}, {}
        out = None
        for s in range(LRU_BLOCKS + 2):
            if s < LRU_BLOCKS:
                ys[s] = in_proj(s)
            if 0 <= s - 1 < LRU_BLOCKS:
                cg[s - 1] = conv_gates(s - 1, ys[s - 1][1])
            if 0 <= s - 2 < LRU_BLOCKS:
                part = recur_out(s - 2, ys.pop(s - 2)[0], *cg.pop(s - 2))
                out = part if out is None else out + part
        o_ref[...] = (x + (out + bout_ref[...])).reshape(BATCH, LRU_STEPS, D_MODEL)


def _rglru(h, norm_g, w_in, b_in, conv_w, conv_b, gate_w, gate_b, lam, w_out, b_out):
    tm = LRU_ROWS
    R = LRU_WIDTH
    BW = LRU_BLOCK_WIDTH
    row_spec = pl.BlockSpec((BATCH, LRU_STEPS, D_MODEL),
                            lambda s: (0, jnp.maximum(s - LRU_PROLOGUE, 0), 0))
    gate_spec = pl.BlockSpec((None, None, BW, 2 * BW),
                             lambda s: (0, jnp.minimum(s, LRU_BLOCKS - 1), 0, 0))
    est = ((D_MODEL * 2 * R + R * D_MODEL + LRU_BLOCKS * BW * 2 * BW) * 2
           + 2 * (D_MODEL * W_CHUNK + BW * 2 * BW + W_CHUNK * D_MODEL) * 4
           + 6 * tm * D_MODEL * 4 + 16 * tm * BW * 4)
    return pl.pallas_call(
        _rglru_body,
        grid=(LRU_PROLOGUE + SEQ // LRU_STEPS,),
        in_specs=[row_spec, _resident((1, D_MODEL)), _col_chunks(0, D_MODEL, LRU_IN_CHUNKS),
                  _resident((1, 2 * R)), _resident((CONV_WIDTH, R)), _resident((1, R)),
                  gate_spec, _resident((LRU_BLOCKS, 1, 2 * BW)),
                  _resident((1, R)), _row_chunks(0, D_MODEL, LRU_BLOCKS), _resident((1, D_MODEL))],
        out_specs=row_spec,
        out_shape=jax.ShapeDtypeStruct((BATCH, SEQ, D_MODEL), F32),
        scratch_shapes=[pltpu.VMEM((LRU_IN_CHUNKS, D_MODEL, W_CHUNK), BF16),
                        pltpu.VMEM((LRU_BLOCKS, BW, 2 * BW), BF16),
                        pltpu.VMEM((LRU_BLOCKS, W_CHUNK, D_MODEL), BF16),
                        pltpu.VMEM((CONV_HIST, R), F32), pltpu.VMEM((BATCH, R), F32)],
        compiler_params=pltpu.CompilerParams(
            dimension_semantics=("arbitrary",), vmem_limit_bytes=_vmem_limit(est)),
        name="rglru",
    )(h, norm_g.reshape(1, D_MODEL), w_in, b_in.reshape(1, 2 * R), conv_w,
      conv_b.reshape(1, R), gate_w, gate_b.reshape(LRU_BLOCKS, 1, 2 * BW),
      lam.reshape(1, R), w_out, b_out.reshape(1, D_MODEL))


ATT_Q = 256
ATT_SUM_ROWS = V7X_BF16_SUBLANES
ATT_Q_SCALE = math.log2(math.e) * QK_DIM ** -0.5
ATT_AHEAD = 2
ATT_HEADS = 2
assert SEQ % ATT_Q == 0 and N_HEADS % ATT_HEADS == 0


def _attn_body(q_ref, kv_ref, lam_ref, sg_ref, o_ref, vt_scr, *, lambda_init):
    tq = ATT_Q
    n_blocks = SEQ // tq
    nt = (((1,), (1,)), ((), ()))
    neg = jnp.finfo(F32).min
    lp = lam_ref[...]
    lam = (jnp.exp(jnp.sum(lp[0:1] * lp[1:2], axis=1, keepdims=True))
           - jnp.exp(jnp.sum(lp[2:3] * lp[3:4], axis=1, keepdims=True)) + lambda_init)
    lane = lax.broadcasted_iota(jnp.int32, (tq, V7X_LANES), 1)
    kpos = lax.broadcasted_iota(jnp.int32, (tq, 2 * tq), 0)
    qpos = lax.broadcasted_iota(jnp.int32, (tq, 2 * tq), 1)
    diag_mask = kpos <= jnp.where(qpos >= tq, qpos - tq, qpos)
    ones_row = lax.broadcasted_iota(jnp.int32, (ATT_SUM_ROWS, SEQ), 0) == 0

    def block(i):
        return slice(i * tq, (i + 1) * tq)

    for g in range(ATT_HEADS):
        v_lanes = slice(g * HEAD_KV_LANES + V7X_LANES, (g + 1) * HEAD_KV_LANES)
        for j in range(n_blocks):
            vt_scr[g, 0:V_DIM, block(j)] = kv_ref[block(j), v_lanes].T
        vt_scr[g, V_DIM:V_DIM + ATT_SUM_ROWS, :] = jnp.where(ones_row, 1.0, 0.0).astype(BF16)

    chains = [[(g, i, j) for i in (range(n_blocks) if g % 2 == 0 else reversed(range(n_blocks)))
               for j in range(i + 1)] for g in range(ATT_HEADS)]
    steps = [st for k in range(len(chains[0])) for ch in chains for st in ch[k:k + 1]]

    def stacked_queries(g, i):
        q = q_ref[block(i), g * HEAD_QK_LANES:(g + 1) * HEAD_QK_LANES]
        zero = jnp.zeros_like(q)
        return jnp.concatenate([jnp.where(lane < QK_DIM, q, zero),
                                jnp.where(lane >= QK_DIM, q, zero)], axis=0)

    def scores(qq, g, i, j):
        k12 = kv_ref[block(j), g * HEAD_KV_LANES:g * HEAD_KV_LANES + V7X_LANES]
        s = lax.dot_general(k12, qq, nt, preferred_element_type=F32)
        return jnp.where(diag_mask, s, neg) if j == i else s

    qq_of, s_of, m_of, acc_of = {}, {}, {}, {}

    def issue_scores(n):
        if n < len(steps):
            g, i, j = steps[n]
            if j == 0:
                qq_of[g, i] = stacked_queries(g, i)
            s_of[n] = scores(qq_of[g, i], g, i, j)
            if j == i:
                del qq_of[g, i]

    for n in range(ATT_AHEAD):
        issue_scores(n)
    for n, (g, i, j) in enumerate(steps):
        issue_scores(n + ATT_AHEAD)
        s = s_of.pop(n)
        block_max = jnp.max(s, axis=0, keepdims=True)
        m_new = block_max if j == 0 else jnp.maximum(m_of[g, i], block_max)
        p = jnp.exp2(s - m_new).astype(BF16)
        pv = _dot(vt_scr[g, :, block(j)], p)
        acc_of[g, i] = pv if j == 0 else jnp.exp2(m_of[g, i] - m_new) * acc_of[g, i] + pv
        m_of[g, i] = m_new
        if j == i:
            acc = acc_of.pop((g, i))
            del m_of[g, i]
            on = acc[0:V_DIM] * (1.0 / acc[V_DIM:V_DIM + 1])
            o = (on[:, :tq] - lam * on[:, tq:]).T
            o = _rmsnorm(o, sg_ref[...]) * (1.0 - lambda_init)
            o_ref[block(i), g * V_DIM:(g + 1) * V_DIM] = o.astype(BF16)


def _attention(q, kv, lam_params, subln_g, lambda_init):
    tq = ATT_Q
    est = (ATT_HEADS * (2 * SEQ * HEAD_QK_LANES * 2 * 2 + 2 * SEQ * HEAD_KV_LANES * 2
                        + (V_DIM + ATT_SUM_ROWS) * SEQ * 2) + 32 * 2 * tq * tq * 4)
    return pl.pallas_call(
        functools.partial(_attn_body, lambda_init=lambda_init),
        grid=(BATCH, N_HEADS // ATT_HEADS),
        in_specs=[pl.BlockSpec((None, SEQ, ATT_HEADS * HEAD_QK_LANES), lambda b, h: (b, 0, h)),
                  pl.BlockSpec((None, SEQ, ATT_HEADS * HEAD_KV_LANES), lambda b, h: (b, 0, h)),
                  _resident((4, QK_DIM)), _resident((1, V_DIM))],
        out_specs=pl.BlockSpec((None, SEQ, ATT_HEADS * V_DIM), lambda b, h: (b, 0, h)),
        out_shape=jax.ShapeDtypeStruct((BATCH, SEQ, N_HEADS * V_DIM), BF16),
        scratch_shapes=[pltpu.VMEM((ATT_HEADS, V_DIM + ATT_SUM_ROWS, SEQ), BF16)],
        compiler_params=pltpu.CompilerParams(
            dimension_semantics=("arbitrary", "arbitrary"), vmem_limit_bytes=_vmem_limit(est)),
        name="diff_attn",
    )(q, kv, lam_params, subln_g.reshape(1, V_DIM))


def kernel(x, ffn1_norm, ffn1_w_in, ffn1_w_out, mix_norm, ffn2_norm, ffn2_w_in, ffn2_w_out,
           rg_w_in, rg_b_in, rg_conv_w, rg_conv_b, rg_gate_w, rg_gate_b, rg_lambda, rg_w_out, rg_b_out,
           kv_norm, w_kv, diff_w_q, diff_lambda, diff_subln, diff_w_o, final_norm):
    assert x.shape == (BATCH, SEQ, D_MODEL) and DEPTH == 2 and N_A_LAYERS == 1
    rows = BATCH * SEQ
    tables = _rope_tables()

    h = x.reshape(rows, D_MODEL)
    h = _ffn(h, 0, ffn1_norm[0], ffn1_w_in, ffn1_w_out)
    h = _rglru(h.reshape(BATCH, SEQ, D_MODEL), mix_norm[0], rg_w_in, rg_b_in[0], rg_conv_w[0],
               rg_conv_b[0], rg_gate_w, rg_gate_b[0], rg_lambda[0], rg_w_out, rg_b_out[0])
    h = h.reshape(rows, D_MODEL)
    h, kv = _ffn(h, 0, ffn2_norm[0], ffn2_w_in, ffn2_w_out,
                 post=(kv_norm, w_kv[None], tables, HEAD_KV_LANES, 1.0))

    lambda_init = 0.8 - 0.6 * math.exp(-0.3 * 1)
    h, q = _ffn(h, 1, ffn1_norm[1], ffn1_w_in, ffn1_w_out,
                post=(mix_norm[1], diff_w_q, tables, HEAD_QK_LANES, ATT_Q_SCALE))
    att = _attention(q.reshape(BATCH, SEQ, N_HEADS * HEAD_QK_LANES),
                     kv.reshape(BATCH, SEQ, KV_WIDTH), diff_lambda[0], diff_subln[0], lambda_init)
    h = _ffn(h, 1, ffn2_norm[1], ffn2_w_in, ffn2_w_out,
             pre=(att.reshape(rows, N_HEADS * V_DIM), diff_w_o), final_g=final_norm)
    return h.reshape(BATCH, SEQ, D_MODEL)
```
